```python
import math
import jax, jax.numpy as jnp
from jax import lax
import numpy as np

D_MODEL = 1024
BATCH = 2
SEQ = 16384
DEPTH = 1

N_HEADS = 8
HEAD_DIM = D_MODEL // N_HEADS
ATTN_WIDTH = N_HEADS * HEAD_DIM
CONV_WIDTH = D_MODEL
CONV_KERNEL = 31
MOBA_BLOCK = 256
MOBA_TOPK = 3
Q_CHUNK = 64
D_FF = -(-8 * D_MODEL // (3 * 256)) * 256
EPS = 1e-6
IN_SPLITS = (CONV_WIDTH, CONV_WIDTH, ATTN_WIDTH, ATTN_WIDTH, ATTN_WIDTH, D_MODEL, D_MODEL)
IN_WIDTH = sum(IN_SPLITS)

kernel_name = 'hybrid_conformer_conv_moba_swiglu'


def rmsnorm(x, g):
    xf = x.astype(jnp.float32)
    y = xf * lax.rsqrt(jnp.mean(xf * xf, axis=-1, keepdims=True) + EPS)
    return (y * g.astype(jnp.float32)).astype(x.dtype)


def layernorm(x, g, b):
    xf = x.astype(jnp.float32)
    mu = jnp.mean(xf, axis=-1, keepdims=True)
    xc = xf - mu
    y = xc * lax.rsqrt(jnp.mean(xc * xc, axis=-1, keepdims=True) + EPS)
    return (y * g.astype(jnp.float32) + b.astype(jnp.float32)).astype(x.dtype)


def alibi_slopes(n_heads):
    return jnp.exp2(-8.0 * jnp.arange(1, n_heads + 1, dtype=jnp.float32) / n_heads)


def conformer_conv(u, gate, dw_w, dw_b, ln_g, ln_b, w_pw2):
    a = u * jax.nn.sigmoid(gate)
    c = a.shape[-1]
    y = lax.conv_general_dilated(
        a, dw_w.astype(a.dtype)[:, None, :], window_strides=(1,),
        padding=[(CONV_KERNEL - 1, 0)], dimension_numbers=('NWC', 'WIO', 'NWC'),
        feature_group_count=c) + dw_b
    y = jax.nn.silu(layernorm(y, ln_g, ln_b))
    return y @ w_pw2


def moba_attention(q, k, v, qn_g, kn_g):
    B, S, H, Dh = q.shape
    L = MOBA_BLOCK
    nb = S // L
    nc = S // Q_CHUNK
    ksel = min(MOBA_TOPK, nb)
    q = rmsnorm(q, qn_g).transpose(0, 2, 1, 3)
    k = rmsnorm(k, kn_g).transpose(0, 2, 1, 3)
    v = v.transpose(0, 2, 1, 3)
    kb = k.reshape(B, H, nb, L, Dh)
    vb = v.reshape(B, H, nb, L, Dh)
    kmean = jnp.mean(kb.astype(jnp.float32), axis=3)
    slopes = alibi_slopes(H)
    sl4 = slopes.reshape(1, H, 1, 1)
    sl5 = slopes.reshape(1, H, 1, 1, 1)
    scale = Dh ** -0.5
    bi = jnp.arange(B)[:, None, None, None]
    hi = jnp.arange(H)[None, :, None, None]
    blk_ids = jnp.arange(nb)
    offs = jnp.arange(L)
    q_chunks = q.reshape(B, H, nc, Q_CHUNK, Dh).transpose(2, 0, 1, 3, 4)

    def chunk(args):
        ci, qc = args
        t = ci * Q_CHUNK + jnp.arange(Q_CHUNK)
        cur = (ci * Q_CHUNK) // L
        gate = jnp.einsum('bhqd,bhnd->bhqn', qc.astype(jnp.float32), kmean)
        gate = jnp.where(blk_ids < cur, gate, -jnp.inf)
        _, idx = lax.top_k(gate, ksel)
        sel_ok = idx < cur
        k_sel = kb[bi, hi, idx]
        v_sel = vb[bi, hi, idx]
        s_past = jnp.einsum('bhqd,bhqkld->bhqkl', qc, k_sel).astype(jnp.float32) * scale
        pos_past = idx[..., None] * L + offs
        dist_past = (t[:, None, None] - pos_past).astype(jnp.float32)
        s_past = jnp.where(sel_ok[..., None], s_past - sl5 * dist_past, -jnp.inf)
        s_past = s_past.reshape(B, H, Q_CHUNK, ksel * L)
        k_own = lax.dynamic_index_in_dim(kb, cur, axis=2, keepdims=False)
        v_own = lax.dynamic_index_in_dim(vb, cur, axis=2, keepdims=False)
        s_own = jnp.einsum('bhqd,bhld->bhql', qc, k_own).astype(jnp.float32) * scale
        dist_own = t[:, None] - (cur * L + offs)[None, :]
        s_own = jnp.where(dist_own >= 0, s_own - sl4 * dist_own.astype(jnp.float32), -jnp.inf)
        p = jax.nn.softmax(jnp.concatenate([s_past, s_own], axis=-1), axis=-1)
        p_past = p[..., :ksel * L].reshape(B, H, Q_CHUNK, ksel, L).astype(v.dtype)
        p_own = p[..., ksel * L:].astype(v.dtype)
        return (jnp.einsum('bhqkl,bhqkld->bhqd', p_past, v_sel)
                + jnp.einsum('bhql,bhld->bhqd', p_own, v_own))

    o = lax.map(chunk, (jnp.arange(nc), q_chunks))
    return o.transpose(1, 0, 3, 2, 4).reshape(B, S, H * Dh)


def swiglu(x, w_gate, w_up, w_down):
    return (jax.nn.silu(x @ w_gate) * (x @ w_up)) @ w_down


def setup_inputs(seed: int = 0) -> dict:
    key = jax.random.key(seed)
    ks = jax.random.split(key, 20)
    f32 = jnp.float32

    def nrm(k, shape, fan_in):
        return jax.random.normal(k, shape, f32) * (fan_in ** -0.5)

    def gain(k, shape):
        return 1.0 + 0.01 * jax.random.normal(k, shape, f32)

    def bias(k, shape):
        return 0.01 * jax.random.normal(k, shape, f32)

    return {
        'x': jax.random.normal(ks[0], (BATCH, SEQ, D_MODEL), f32),
        'norm1_g': gain(ks[1], (DEPTH, D_MODEL)),
        'w_in': nrm(ks[2], (DEPTH, D_MODEL, IN_WIDTH), D_MODEL),
        'dw_w': nrm(ks[3], (DEPTH, CONV_KERNEL, CONV_WIDTH), CONV_KERNEL),
        'dw_b': bias(ks[4], (DEPTH, CONV_WIDTH)),
        'conv_ln_g': gain(ks[5], (DEPTH, CONV_WIDTH)),
        'conv_ln_b': bias(ks[6], (DEPTH, CONV_WIDTH)),
        'w_conv_out': nrm(ks[7], (DEPTH, CONV_WIDTH, D_MODEL), CONV_WIDTH),
        'q_norm_g': gain(ks[8], (DEPTH, HEAD_DIM)),
        'k_norm_g': gain(ks[9], (DEPTH, HEAD_DIM)),
        'w_attn_out': nrm(ks[10], (DEPTH, ATTN_WIDTH, D_MODEL), ATTN_WIDTH),
        'w_out': nrm(ks[11], (DEPTH, D_MODEL, D_MODEL), D_MODEL),
        'norm2_g': gain(ks[12], (DEPTH, D_MODEL)),
        'w_ffn_gate': nrm(ks[13], (DEPTH, D_MODEL, D_FF), D_MODEL),
        'w_ffn_up': nrm(ks[14], (DEPTH, D_MODEL, D_FF), D_MODEL),
        'w_ffn_down': nrm(ks[15], (DEPTH, D_FF, D_MODEL), D_FF),
    }


def reference(x, norm1_g, w_in, dw_w, dw_b, conv_ln_g, conv_ln_b, w_conv_out,
              q_norm_g, k_norm_g, w_attn_out, w_out, norm2_g,
              w_ffn_gate, w_ffn_up, w_ffn_down):
    B, S, _ = x.shape
    cuts = [int(c) for c in np.cumsum(IN_SPLITS)[:-1]]
    h = x
    for l in range(DEPTH):
        n = rmsnorm(h, norm1_g[l])
        z = n @ w_in[l]
        cu, cg, q, k, v, gc, ga = jnp.split(z, cuts, axis=-1)
        y_conv = conformer_conv(cu, cg, dw_w[l], dw_b[l], conv_ln_g[l], conv_ln_b[l], w_conv_out[l])
        att = moba_attention(q.reshape(B, S, N_HEADS, HEAD_DIM),
                             k.reshape(B, S, N_HEADS, HEAD_DIM),
                             v.reshape(B, S, N_HEADS, HEAD_DIM),
                             q_norm_g[l], k_norm_g[l])
        y_attn = att @ w_attn_out[l]
        merged = jax.nn.sigmoid(gc) * y_conv + jax.nn.sigmoid(ga) * y_attn
        h = h + merged @ w_out[l]
        h = h + swiglu(rmsnorm(h, norm2_g[l]), w_ffn_gate[l], w_ffn_up[l], w_ffn_down[l])
    return h
```

```python
import functools

import jax
import jax.numpy as jnp
from jax import lax
from jax.experimental import pallas as pl
from jax.experimental.pallas import tpu as pltpu

F32 = jnp.float32
BF16 = jnp.bfloat16

N_HEADS = 8
MOBA_BLOCK = 256
MOBA_TOPK = 3
EPS = 1e-6
MASKED = -1e30
M_INIT = -1e29

VMEM_LIMIT = 56 * 1024 * 1024


def _const_spec(shape):
    nd = len(shape)
    return pl.BlockSpec(shape, lambda *_: (0,) * nd, pipeline_mode=pl.Buffered(1))


def _alibi_slope(h):
    return 2.0 ** (-8.0 * (h + 1) / N_HEADS)


def _in_proj_kernel(x_ref, g1_ref, w_ref, qg_ref, kg_ref,
                    a_ref, q_ref, k_ref, v_ref, sgc_ref, sga_ref, kmean_ref,
                    *, d, dh, scale):
    x = x_ref[...]
    n = x * lax.rsqrt(jnp.mean(x * x, axis=-1, keepdims=True) + EPS) * g1_ref[...]
    n = n.astype(BF16)

    def proj(c):
        return jnp.dot(n, w_ref[:, c * d:(c + 1) * d], preferred_element_type=F32)

    cu = proj(0)
    cg = proj(1)
    a_ref[...] = (cu * jax.nn.sigmoid(cg)).astype(BF16)

    q = proj(2)
    for h in range(N_HEADS):
        qh = q[:, h * dh:(h + 1) * dh]
        qh = qh * lax.rsqrt(jnp.mean(qh * qh, axis=-1, keepdims=True) + EPS) * qg_ref[...]
        q_ref[:, h * dh:(h + 1) * dh] = (qh * scale).astype(BF16)

    k = proj(3)
    tm = k.shape[0]
    for h in range(N_HEADS):
        kh = k[:, h * dh:(h + 1) * dh]
        kh = kh * lax.rsqrt(jnp.mean(kh * kh, axis=-1, keepdims=True) + EPS) * kg_ref[...]
        k_ref[:, h * dh:(h + 1) * dh] = kh.astype(BF16)
        kmean_ref[0, :, h * dh:(h + 1) * dh] = jnp.mean(
            kh.reshape(tm // MOBA_BLOCK, MOBA_BLOCK, dh), axis=1)

    v_ref[...] = proj(4).astype(BF16)
    sgc_ref[...] = jax.nn.sigmoid(proj(5)).astype(BF16)
    sga_ref[...] = jax.nn.sigmoid(proj(6)).astype(BF16)


def _in_proj(x2, g1, w_in, qg, kg, *, tm):
    t, d = x2.shape
    dh = d // N_HEADS
    nbt = tm // MOBA_BLOCK
    row = pl.BlockSpec((tm, d), lambda i: (i, 0))
    out_bf = jax.ShapeDtypeStruct((t, d), BF16)
    return pl.pallas_call(
        functools.partial(_in_proj_kernel, d=d, dh=dh, scale=dh ** -0.5),
        grid=(t // tm,),
        in_specs=[row, _const_spec((1, d)), _const_spec(w_in.shape),
                  _const_spec((1, dh)), _const_spec((1, dh))],
        out_specs=[row, row, row, row, row, row,
                   pl.BlockSpec((1, nbt, d), lambda i: (i, 0, 0))],
        out_shape=[out_bf] * 6 + [jax.ShapeDtypeStruct((t // tm, nbt, d), F32)],
        compiler_params=pltpu.CompilerParams(
            dimension_semantics=("arbitrary",), vmem_limit_bytes=VMEM_LIMIT),
        name="in_proj",
    )(x2, g1, w_in, qg, kg)


def _attn_prep_kernel(q_ref, k_ref, v_ref, kmean_ref, qt_ref, kf_ref, vt_ref,
                      *, dh, nb, tp):
    i = pl.program_id(1)
    base = i * tp
    nsub = tp // MOBA_BLOCK
    nx = 2 * dh - dh

    eye = (lax.broadcasted_iota(jnp.int32, (dh, dh), 0)
           == lax.broadcasted_iota(jnp.int32, (dh, dh), 1)).astype(BF16)
    nt = (((1,), (1,)), ((), ()))

    blk = lax.broadcasted_iota(jnp.int32, (nb, tp), 0)
    qpos = base + lax.broadcasted_iota(jnp.int32, (nb, tp), 1)
    cur = qpos // MOBA_BLOCK
    past = blk < cur
    xrow = lax.broadcasted_iota(jnp.int32, (nx - nb, tp), 0)
    xpos = base + lax.broadcasted_iota(jnp.int32, (nx - nb, tp), 1)
    xcur = (xpos // MOBA_BLOCK).astype(F32)
    xrem = (xpos % MOBA_BLOCK).astype(F32)

    kcol = lax.broadcasted_iota(jnp.int32, (tp, nx), 1)
    kpos = base + lax.broadcasted_iota(jnp.int32, (tp, nx), 0)
    kblk = kpos // MOBA_BLOCK
    kblk_f = kblk.astype(F32)
    krem_f = (kpos % MOBA_BLOCK).astype(F32)

    for h in range(N_HEADS):
        slope = _alibi_slope(h)
        hs = slice(h * dh, (h + 1) * dh)
        qh = q_ref[0, :, hs]
        km = kmean_ref[0, :, hs]
        km_hi = km.astype(BF16)
        r1 = km - km_hi.astype(F32)
        km_mid = r1.astype(BF16)
        km_lo = (r1 - km_mid.astype(F32)).astype(BF16)
        lhs = jnp.concatenate([km_hi, km_mid, km_lo, eye], axis=0)
        r = lax.dot_general(lhs, qh, nt, preferred_element_type=F32)
        gate = r[0:nb] + r[nb:2 * nb] + r[2 * nb:3 * nb]
        q_t = r[3 * nb:3 * nb + dh]

        g = jnp.where(past, gate, -jnp.inf)
        sel = jnp.zeros((nb, tp), jnp.bool_)
        for _ in range(min(MOBA_TOPK, nb)):
            mx = jnp.max(g, axis=0, keepdims=True)
            first = jnp.min(jnp.where(g == mx, blk, nb), axis=0, keepdims=True)
            pick = blk == first
            sel = jnp.logical_or(sel, pick)
            g = jnp.where(pick, -jnp.inf, g)
        keep = jnp.logical_or(jnp.logical_and(sel, past), blk == cur)
        selbias = jnp.where(keep, 0.0, MASKED).astype(F32)

        xq = jnp.where(xrow == 0, -slope * MOBA_BLOCK * xcur,
             jnp.where(xrow == 1, -slope * xrem,
             jnp.where(xrow < 4, 1.0, 0.0)))
        qt_ref[0, h] = jnp.concatenate([q_t, selbias, xq], axis=0).astype(BF16)

        kx = jnp.where(kcol < nb, (kcol == kblk).astype(F32),
             jnp.where(kcol < nb + 2, 1.0,
             jnp.where(kcol == nb + 2, slope * MOBA_BLOCK * kblk_f,
             jnp.where(kcol == nb + 3, slope * krem_f, 0.0)))).astype(BF16)
        kh = k_ref[0, :, hs]
        v_t = lax.dot_general(eye, v_ref[0, :, hs], nt,
                              preferred_element_type=F32).astype(BF16)
        for s in range(nsub):
            rs = slice(s * MOBA_BLOCK, (s + 1) * MOBA_BLOCK)
            kf_ref[0, h, s, :, 0:dh] = kh[rs]
            kf_ref[0, h, s, :, dh:2 * dh] = kx[rs]
            vt_ref[0, h, s] = v_t[:, rs]


def _attn_prep(q, k, v, kmean, *, tp):
    b, s, d = q.shape
    dh = d // N_HEADS
    nb = s // MOBA_BLOCK
    nsub = tp // MOBA_BLOCK
    assert dh + nb + 4 <= 2 * dh
    row = pl.BlockSpec((1, tp, d), lambda bi, i: (bi, i, 0))
    return pl.pallas_call(
        functools.partial(_attn_prep_kernel, dh=dh, nb=nb, tp=tp),
        grid=(b, s // tp),
        in_specs=[row, row, row, pl.BlockSpec((1, nb, d), lambda bi, i: (bi, 0, 0))],
        out_specs=[
            pl.BlockSpec((1, N_HEADS, 2 * dh, tp), lambda bi, i: (bi, 0, 0, i)),
            pl.BlockSpec((1, N_HEADS, nsub, MOBA_BLOCK, 2 * dh),
                         lambda bi, i: (bi, 0, i, 0, 0)),
            pl.BlockSpec((1, N_HEADS, nsub, dh, MOBA_BLOCK),
                         lambda bi, i: (bi, 0, i, 0, 0)),
        ],
        out_shape=[
            jax.ShapeDtypeStruct((b, N_HEADS, 2 * dh, s), BF16),
            jax.ShapeDtypeStruct((b, N_HEADS, nb, MOBA_BLOCK, 2 * dh), BF16),
            jax.ShapeDtypeStruct((b, N_HEADS, nb, dh, MOBA_BLOCK), BF16),
        ],
        compiler_params=pltpu.CompilerParams(
            dimension_semantics=("arbitrary", "arbitrary"),
            vmem_limit_bytes=VMEM_LIMIT),
        name="attn_prep",
    )(q, k, v, kmean)


def _moba_attn_kernel(qt_ref, kf_ref, vt_ref, o_ref, *, dh, tq):
    i = pl.program_id(2)
    qt = qt_ref[0, 0]

    def update(s, j, carry):
        m, l, acc = carry
        m_new = jnp.maximum(m, jnp.max(s, axis=0, keepdims=True))
        alpha = jnp.exp(m - m_new)
        p = jnp.exp(s - m_new)
        l = alpha * l + jnp.sum(p, axis=0, keepdims=True)
        pv = jnp.dot(vt_ref[0, 0, j], p.astype(BF16), preferred_element_type=F32)
        return m_new, l, alpha * acc + pv

    def past_block(j, carry):
        s = jnp.dot(kf_ref[0, 0, j], qt, preferred_element_type=F32)
        return update(s, j, carry)

    init = (jnp.full((1, tq), M_INIT, F32), jnp.zeros((1, tq), F32),
            jnp.zeros((dh, tq), F32))
    carry = lax.fori_loop(0, i, past_block, init)

    s = jnp.dot(kf_ref[0, 0, i], qt, preferred_element_type=F32)
    kpos = lax.broadcasted_iota(jnp.int32, (MOBA_BLOCK, tq), 0)
    qpos = lax.broadcasted_iota(jnp.int32, (MOBA_BLOCK, tq), 1)
    s = jnp.where(kpos <= qpos, s, MASKED)
    _, l, acc = update(s, i, carry)
    o_ref[0] = (acc / l).T.astype(o_ref.dtype)


def _moba_attn(qt, kf, vt, *, s, d):
    b, nh, dq, _ = qt.shape
    nb = kf.shape[2]
    dh = d // nh
    tq = MOBA_BLOCK
    return pl.pallas_call(
        functools.partial(_moba_attn_kernel, dh=dh, tq=tq),
        grid=(b, nh, s // tq),
        in_specs=[
            pl.BlockSpec((1, 1, dq, tq), lambda bi, h, i: (bi, h, 0, i)),
            pl.BlockSpec((1, 1, nb, MOBA_BLOCK, dq), lambda bi, h, i: (bi, h, 0, 0, 0)),
            pl.BlockSpec((1, 1, nb, dh, MOBA_BLOCK), lambda bi, h, i: (bi, h, 0, 0, 0)),
        ],
        out_specs=pl.BlockSpec((1, tq, dh), lambda bi, h, i: (bi, i, h)),
        out_shape=jax.ShapeDtypeStruct((b, s, d), BF16),
        compiler_params=pltpu.CompilerParams(
            dimension_semantics=("arbitrary", "arbitrary", "arbitrary"),
            vmem_limit_bytes=VMEM_LIMIT),
        name="moba_attn",
    )(qt, kf, vt)


def _conv_merge_kernel(a_ref, halo_ref, dww_ref, dwb_ref, lng_ref, lnb_ref,
                       wco_ref, att_ref, wao_ref, sgc_ref, sga_ref, wo_ref,
                       x_ref, o_ref, win_ref, *, tm, kc, halo):
    i = pl.program_id(1)
    hl = halo_ref[0].astype(F32)
    win_ref[0:halo] = jnp.where(i > 0, hl, 0.0)
    win_ref[halo:halo + tm] = a_ref[0].astype(F32)

    y = jnp.zeros((tm, a_ref.shape[-1]), F32) + dwb_ref[...]
    off = halo - (kc - 1)
    for t in range(kc):
        y = y + win_ref[off + t:off + t + tm, :] * dww_ref[t:t + 1, :]

    mu = jnp.mean(y, axis=-1, keepdims=True)
    yc = y - mu
    yn = yc * lax.rsqrt(jnp.mean(yc * yc, axis=-1, keepdims=True) + EPS)
    yn = yn * lng_ref[...] + lnb_ref[...]
    act = (yn * jax.nn.sigmoid(yn)).astype(BF16)
    y_conv = jnp.dot(act, wco_ref[...], preferred_element_type=F32)
    y_attn = jnp.dot(att_ref[0], wao_ref[...], preferred_element_type=F32)
    merged = sgc_ref[0].astype(F32) * y_conv + sga_ref[0].astype(F32) * y_attn
    o_ref[0] = x_ref[0] + jnp.dot(merged.astype(BF16), wo_ref[...],
                                  preferred_element_type=F32)


def _conv_merge(a, dw_w, dw_b, ln_g, ln_b, w_co, att, w_ao, sgc, sga, w_o, x, *, tm):
    b, s, d = x.shape
    kc = dw_w.shape[0]
    halo = 32
    assert kc - 1 <= halo and tm % halo == 0
    row = pl.BlockSpec((1, tm, d), lambda bi, i: (bi, i, 0))
    halo_spec = pl.BlockSpec(
        (1, halo, d), lambda bi, i: (bi, jnp.maximum(i * (tm // halo) - 1, 0), 0))
    vec = _const_spec((1, d))
    return pl.pallas_call(
        functools.partial(_conv_merge_kernel, tm=tm, kc=kc, halo=halo),
        grid=(b, s // tm),
        in_specs=[row, halo_spec, _const_spec(dw_w.shape), vec, vec, vec,
                  _const_spec(w_co.shape), row, _const_spec(w_ao.shape), row, row,
                  _const_spec(w_o.shape), row],
        out_specs=row,
        out_shape=jax.ShapeDtypeStruct((b, s, d), F32),
        scratch_shapes=[pltpu.VMEM((tm + halo, d), F32)],
        compiler_params=pltpu.CompilerParams(
            dimension_semantics=("arbitrary", "arbitrary"),
            vmem_limit_bytes=VMEM_LIMIT),
        name="conv_merge",
    )(a, a, dw_w, dw_b, ln_g, ln_b, w_co, att, w_ao, sgc, sga, w_o, x)


def _ffn_kernel(h_ref, g2_ref, wg_ref, wu_ref, wd_ref, o_ref, *, chunks):
    h = h_ref[...]
    n = h * lax.rsqrt(jnp.mean(h * h, axis=-1, keepdims=True) + EPS) * g2_ref[...]
    n = n.astype(BF16)
    out = h
    for lo, hi in chunks:
        g = jnp.dot(n, wg_ref[:, lo:hi], preferred_element_type=F32)
        u = jnp.dot(n, wu_ref[:, lo:hi], preferred_element_type=F32)
        act = (g * jax.nn.sigmoid(g) * u).astype(BF16)
        out = out + jnp.dot(act, wd_ref[lo:hi, :], preferred_element_type=F32)
    o_ref[...] = out


def _ffn(h2, g2, w_g, w_u, w_d, *, tm):
    t, d = h2.shape
    dff = w_g.shape[1]
    step = 768
    chunks = tuple((lo, min(lo + step, dff)) for lo in range(0, dff, step))
    row = pl.BlockSpec((tm, d), lambda i: (i, 0))
    return pl.pallas_call(
        functools.partial(_ffn_kernel, chunks=chunks),
        grid=(t // tm,),
        in_specs=[row, _const_spec((1, d)), _const_spec(w_g.shape),
                  _const_spec(w_u.shape), _const_spec(w_d.shape)],
        out_specs=row,
        out_shape=jax.ShapeDtypeStruct((t, d), F32),
        compiler_params=pltpu.CompilerParams(
            dimension_semantics=("arbitrary",), vmem_limit_bytes=VMEM_LIMIT),
        name="ffn",
    )(h2, g2, w_g, w_u, w_d)


def _layer(x, norm1_g, w_in, dw_w, dw_b, conv_ln_g, conv_ln_b, w_conv_out,
           q_norm_g, k_norm_g, w_attn_out, w_out, norm2_g,
           w_ffn_gate, w_ffn_up, w_ffn_down):
    b, s, d = x.shape
    t = b * s
    tm = 512
    row = lambda p: p.reshape(1, -1)
    bf = lambda w: w.astype(BF16)

    a, q, k, v, sgc, sga, kmean = _in_proj(
        x.reshape(t, d), row(norm1_g), bf(w_in), row(q_norm_g), row(k_norm_g), tm=tm)
    r3 = lambda z: z.reshape(b, s, d)
    kmean = kmean.reshape(b, s // MOBA_BLOCK, d)
    qt, kf, vt = _attn_prep(r3(q), r3(k), r3(v), kmean, tp=tm)
    att = _moba_attn(qt, kf, vt, s=s, d=d)
    h1 = _conv_merge(r3(a), dw_w, row(dw_b), row(conv_ln_g), row(conv_ln_b),
                     bf(w_conv_out), att, bf(w_attn_out), r3(sgc), r3(sga),
                     bf(w_out), x, tm=tm)
    out = _ffn(h1.reshape(t, d), row(norm2_g), bf(w_ffn_gate), bf(w_ffn_up),
               bf(w_ffn_down), tm=tm)
    return out.reshape(b, s, d)


def kernel(x, norm1_g, w_in, dw_w, dw_b, conv_ln_g, conv_ln_b, w_conv_out,
           q_norm_g, k_norm_g, w_attn_out, w_out, norm2_g,
           w_ffn_gate, w_ffn_up, w_ffn_down):
    h = x
    for l in range(norm1_g.shape[0]):
        h = _layer(h, norm1_g[l], w_in[l], dw_w[l], dw_b[l], conv_ln_g[l],
                   conv_ln_b[l], w_conv_out[l], q_norm_g[l], k_norm_g[l],
                   w_attn_out[l], w_out[l], norm2_g[l],
                   w_ffn_gate[l], w_ffn_up[l], w_ffn_down[l])
    return h
```

```python
import functools
import math

import jax
import jax.numpy as jnp
import ml_dtypes
import numpy as np
from jax import lax
from jax.experimental import pallas as pl
from jax.experimental.pallas import tpu as pltpu

F32 = jnp.float32
BF16 = jnp.bfloat16

N_HEADS = 8
MOBA_BLOCK = 256
MOBA_TOPK = 3
EPS = 1e-6
LOG2E = math.log2(math.e)
MASKED = -1e30
M_INIT = -1e29

SUBLANES = 8
LANES = 128
VMEM_LIMIT = 56 * 1024 * 1024
ATTN_TILE = 1024


def _const_spec(shape):
    nd = len(shape)
    return pl.BlockSpec(shape, lambda *_: (0,) * nd, pipeline_mode=pl.Buffered(1))


def _alibi_slope(h):
    return 2.0 ** (-8.0 * (h + 1) / N_HEADS)


def _bf16_pieces(x, n=3):
    pieces = []
    rest = np.float32(x)
    for _ in range(n):
        p = np.float32(rest.astype(ml_dtypes.bfloat16))
        pieces.append(float(p))
        rest = np.float32(rest - p)
    return pieces


def _in_proj_kernel(x_ref, g1_ref, w_ref, qg_ref, kg_ref,
                    a_ref, q_ref, k_ref, v_ref, sgc_ref, sga_ref, kmean_ref,
                    *, d, dh, scale):
    x = x_ref[...]
    n = x * lax.rsqrt(jnp.mean(x * x, axis=-1, keepdims=True) + EPS) * g1_ref[...]
    n = n.astype(BF16)

    def proj(c):
        return jnp.dot(n, w_ref[:, c * d:(c + 1) * d], preferred_element_type=F32)

    cu = proj(0)
    cg = proj(1)
    a_ref[...] = (cu * jax.nn.sigmoid(cg)).astype(BF16)

    q = proj(2)
    for h in range(N_HEADS):
        qh = q[:, h * dh:(h + 1) * dh]
        qh = qh * lax.rsqrt(jnp.mean(qh * qh, axis=-1, keepdims=True) + EPS) * qg_ref[...]
        q_ref[:, h * dh:(h + 1) * dh] = (qh * (scale * LOG2E)).astype(BF16)

    k = proj(3)
    tm = k.shape[0]
    for h in range(N_HEADS):
        kh = k[:, h * dh:(h + 1) * dh]
        kh = kh * lax.rsqrt(jnp.mean(kh * kh, axis=-1, keepdims=True) + EPS) * kg_ref[...]
        k_ref[:, h * dh:(h + 1) * dh] = kh.astype(BF16)
        kmean_ref[0, :, h * dh:(h + 1) * dh] = jnp.mean(
            kh.reshape(tm // MOBA_BLOCK, MOBA_BLOCK, dh), axis=1)

    v_ref[...] = proj(4).astype(BF16)
    sgc_ref[...] = jax.nn.sigmoid(proj(5)).astype(BF16)
    sga_ref[...] = jax.nn.sigmoid(proj(6)).astype(BF16)


def _in_proj(x2, g1, w_in, qg, kg, *, tm):
    t, d = x2.shape
    dh = d // N_HEADS
    nbt = tm // MOBA_BLOCK
    row = pl.BlockSpec((tm, d), lambda i: (i, 0))
    out_bf = jax.ShapeDtypeStruct((t, d), BF16)
    return pl.pallas_call(
        functools.partial(_in_proj_kernel, d=d, dh=dh, scale=dh ** -0.5),
        grid=(t // tm,),
        in_specs=[row, _const_spec((1, d)), _const_spec(w_in.shape),
                  _const_spec((1, dh)), _const_spec((1, dh))],
        out_specs=[row, row, row, row, row, row,
                   pl.BlockSpec((1, nbt, d), lambda i: (i, 0, 0))],
        out_shape=[out_bf] * 6 + [jax.ShapeDtypeStruct((t // tm, nbt, d), F32)],
        compiler_params=pltpu.CompilerParams(
            dimension_semantics=("arbitrary",), vmem_limit_bytes=VMEM_LIMIT),
        name="in_proj",
    )(x2, g1, w_in, qg, kg)


def _attn_prep_kernel(q_ref, k_ref, v_ref, kmean_ref, qt_ref, kf_ref, vt_ref,
                      *, dh, nb, tp):
    i = pl.program_id(1)
    base = i * tp
    nx = dh

    eye = (lax.broadcasted_iota(jnp.int32, (dh, dh), 0)
           == lax.broadcasted_iota(jnp.int32, (dh, dh), 1)).astype(BF16)
    nt = (((1,), (1,)), ((), ()))

    blk = lax.broadcasted_iota(jnp.int32, (nb, tp), 0)
    qpos = base + lax.broadcasted_iota(jnp.int32, (nb, tp), 1)
    cur = qpos // MOBA_BLOCK
    past = blk < cur
    xrow = lax.broadcasted_iota(jnp.int32, (nx - nb, tp), 0)
    xpos = base + lax.broadcasted_iota(jnp.int32, (nx - nb, tp), 1)
    xcur = (xpos // MOBA_BLOCK).astype(F32)
    xrem = (xpos % MOBA_BLOCK).astype(F32)

    kcol = lax.broadcasted_iota(jnp.int32, (tp, nx), 1)
    kpos = base + lax.broadcasted_iota(jnp.int32, (tp, nx), 0)
    kblk = kpos // MOBA_BLOCK
    kblk_f = kblk.astype(F32)
    krem_f = (kpos % MOBA_BLOCK).astype(F32)

    for h in range(N_HEADS):
        hs = slice(h * dh, (h + 1) * dh)
        qh = q_ref[0, :, hs]
        km = kmean_ref[0, :, hs]
        km_hi = km.astype(BF16)
        r1 = km - km_hi.astype(F32)
        km_mid = r1.astype(BF16)
        km_lo = (r1 - km_mid.astype(F32)).astype(BF16)
        lhs = jnp.concatenate([km_hi, km_mid, km_lo, eye], axis=0)
        r = lax.dot_general(lhs, qh, nt, preferred_element_type=F32)
        gate = r[0:nb] + r[nb:2 * nb] + r[2 * nb:3 * nb]
        q_t = r[3 * nb:3 * nb + dh]

        g = jnp.where(past, gate, -jnp.inf)
        sel = jnp.zeros((nb, tp), jnp.bool_)
        for _ in range(min(MOBA_TOPK, nb)):
            mx = jnp.max(g, axis=0, keepdims=True)
            first = jnp.min(jnp.where(g == mx, blk, nb), axis=0, keepdims=True)
            pick = blk == first
            sel = jnp.logical_or(sel, pick)
            g = jnp.where(pick, -jnp.inf, g)
        keep = jnp.logical_or(jnp.logical_and(sel, past), blk == cur)
        selbias = jnp.where(keep, 0.0, MASKED).astype(F32)

        xq = jnp.zeros((nx - nb, tp), F32)
        kx = (kcol == kblk).astype(F32)
        for t, c_t in enumerate(_bf16_pieces(_alibi_slope(h) * LOG2E)):
            xq = jnp.where(xrow == 4 * t, -MOBA_BLOCK * xcur,
                 jnp.where(xrow == 4 * t + 1, -xrem,
                 jnp.where((xrow == 4 * t + 2) | (xrow == 4 * t + 3), c_t, xq)))
            kc = kcol - (nb + 4 * t)
            kx = jnp.where((kc == 0) | (kc == 1), c_t,
                 jnp.where(kc == 2, MOBA_BLOCK * kblk_f,
                 jnp.where(kc == 3, krem_f, kx)))
        kx = kx.astype(BF16)
        qt_ref[0, h] = jnp.concatenate([q_t, selbias, xq], axis=0).astype(BF16)
        kh = k_ref[0, :, hs]
        v_t = lax.dot_general(eye, v_ref[0, :, hs], nt,
                              preferred_element_type=F32).astype(BF16)
        for s in range(tp // MOBA_BLOCK):
            rs = slice(s * MOBA_BLOCK, (s + 1) * MOBA_BLOCK)
            kf_ref[0, h, s, :, 0:dh] = kh[rs]
            kf_ref[0, h, s, :, dh:2 * dh] = kx[rs]
            vt_ref[0, h, s] = v_t[:, rs]


def _attn_prep(q, k, v, kmean, *, tp):
    b, s, d = q.shape
    dh = d // N_HEADS
    nb = s // MOBA_BLOCK
    nsub = tp // MOBA_BLOCK
    assert dh + nb + 12 <= 2 * dh
    row = pl.BlockSpec((1, tp, d), lambda bi, i: (bi, i, 0))
    return pl.pallas_call(
        functools.partial(_attn_prep_kernel, dh=dh, nb=nb, tp=tp),
        grid=(b, s // tp),
        in_specs=[row, row, row, pl.BlockSpec((1, nb, d), lambda bi, i: (bi, 0, 0))],
        out_specs=[
            pl.BlockSpec((1, N_HEADS, 2 * dh, tp), lambda bi, i: (bi, 0, 0, i)),
            pl.BlockSpec((1, N_HEADS, nsub, MOBA_BLOCK, 2 * dh),
                         lambda bi, i: (bi, 0, i, 0, 0)),
            pl.BlockSpec((1, N_HEADS, nsub, dh, MOBA_BLOCK),
                         lambda bi, i: (bi, 0, i, 0, 0)),
        ],
        out_shape=[
            jax.ShapeDtypeStruct((b, N_HEADS, 2 * dh, s), BF16),
            jax.ShapeDtypeStruct((b, N_HEADS, nb, MOBA_BLOCK, 2 * dh), BF16),
            jax.ShapeDtypeStruct((b, N_HEADS, nb, dh, MOBA_BLOCK), BF16),
        ],
        compiler_params=pltpu.CompilerParams(
            dimension_semantics=("arbitrary", "arbitrary"),
            vmem_limit_bytes=VMEM_LIMIT),
        name="attn_prep",
    )(q, k, v, kmean)


def _moba_attn_kernel(qt_ref, kf_ref, vt_ref, o_ref, s0_ref, s1_ref, s2_ref, s3_ref,
                      m_ref, l_ref, acc_ref, *, tq):
    i = pl.program_id(2)
    blk = MOBA_BLOCK
    ngrp = tq // blk
    npast = i * ngrp
    m_ref[...] = jnp.full(m_ref.shape, M_INIT, F32)
    l_ref[...] = jnp.zeros(l_ref.shape, F32)
    acc_ref[...] = jnp.zeros(acc_ref.shape, F32)

    s_refs = (s0_ref, s1_ref, s2_ref, s3_ref)

    def scores(j, c, buf):
        s_refs[buf][c] = jnp.dot(kf_ref[0, 0, j], qt_ref[0, 0, :, c * blk:(c + 1) * blk],
                                 preferred_element_type=F32)

    def absorb(c, items):
        cs = slice(c * blk, (c + 1) * blk)
        tiles = []
        for j, buf, causal in items:
            s = s_refs[buf][c]
            if causal:
                kpos = lax.broadcasted_iota(jnp.int32, s.shape, 0)
                qpos = lax.broadcasted_iota(jnp.int32, s.shape, 1)
                s = jnp.where(kpos <= qpos, s, MASKED)
            tiles.append(s)
        m_old = m_ref[:, cs]
        m_new = m_old
        for s in tiles:
            m_new = jnp.maximum(m_new, jnp.max(s, axis=0, keepdims=True))
        alpha = jnp.exp2(m_old - m_new)
        lsum = None
        pv = None
        for (j, _, _), s in zip(items, tiles):
            p = jnp.exp2(s - m_new)
            ps = jnp.sum(p, axis=0, keepdims=True)
            pj = jnp.dot(vt_ref[0, 0, j], p.astype(BF16), preferred_element_type=F32)
            lsum = ps if lsum is None else lsum + ps
            pv = pj if pv is None else pv + pj
        l_ref[:, cs] = alpha * l_ref[:, cs] + lsum
        acc_ref[:, cs] = alpha * acc_ref[:, cs] + pv
        m_ref[:, cs] = m_new

    for c in range(ngrp):
        scores(0, c, 0)
        scores(1, c, 1)

    def past_quad(t, carry):
        for half in range(2):
            j = 4 * t + 2 * half
            cur, nxt = 2 * half, 2 * (1 - half)
            for c in range(ngrp):
                scores(j + 2, c, nxt)
                scores(j + 3, c, nxt + 1)
                absorb(c, [(j, cur, False), (j + 1, cur + 1, False)])
        return carry

    lax.fori_loop(0, npast // 4, past_quad, 0)

    for c in range(2, ngrp):
        for g in range(2, c + 1):
            scores(npast + g, c, g)
    for c in range(ngrp):
        own = [(npast + g, g, g == c) for g in range(c + 1)]
        for lo in range(0, len(own), 2):
            absorb(c, own[lo:lo + 2])
    o_ref[0] = (acc_ref[...] / l_ref[...]).T.astype(o_ref.dtype)


def _moba_attn(qt, kf, vt, *, s, d, tq):
    b, nh, dq, _ = qt.shape
    nb = kf.shape[2]
    dh = d // nh
    blk = MOBA_BLOCK
    assert tq // blk == 4
    sbuf = pltpu.VMEM((tq // blk, blk, blk), F32)
    return pl.pallas_call(
        functools.partial(_moba_attn_kernel, tq=tq),
        grid=(b, nh, s // tq),
        in_specs=[
            pl.BlockSpec((1, 1, dq, tq), lambda bi, h, i: (bi, h, 0, i)),
            pl.BlockSpec((1, 1, nb, blk, dq), lambda bi, h, i: (bi, h, 0, 0, 0)),
            pl.BlockSpec((1, 1, nb, dh, blk), lambda bi, h, i: (bi, h, 0, 0, 0)),
        ],
        out_specs=pl.BlockSpec((1, tq, dh), lambda bi, h, i: (bi, i, h)),
        out_shape=jax.ShapeDtypeStruct((b, s, d), BF16),
        scratch_shapes=[sbuf, sbuf, sbuf, sbuf,
                        pltpu.VMEM((1, tq), F32), pltpu.VMEM((1, tq), F32),
                        pltpu.VMEM((dh, tq), F32)],
        compiler_params=pltpu.CompilerParams(
            dimension_semantics=("arbitrary", "arbitrary", "arbitrary"),
            vmem_limit_bytes=VMEM_LIMIT),
        name="moba_attn",
    )(qt, kf, vt)


def _conv_merge_kernel(a_ref, halo_ref, dww_ref, dwb_ref, lng_ref, lnb_ref,
                       wco_ref, att_ref, wao_ref, sgc_ref, sga_ref, wo_ref,
                       x_ref, o_ref, win_ref, sh_ref, y_ref, *, tm, kc, halo):
    i = pl.program_id(1)
    hl = halo_ref[0].astype(F32)
    win_ref[0:halo] = jnp.where(i > 0, hl, 0.0)
    win_ref[halo:halo + tm] = a_ref[0].astype(F32)

    off = halo - (kc - 1)
    span = sh_ref.shape[1]
    rows = 64
    for lc in range(a_ref.shape[-1] // LANES):
        ls = slice(lc * LANES, (lc + 1) * LANES)
        for r in range(1, SUBLANES):
            sh_ref[r - 1] = win_ref[r:r + span, ls]
        w_rows = [jnp.broadcast_to(dww_ref[k:k + 1, ls], (rows, LANES)) for k in range(kc)]
        bias = jnp.broadcast_to(dwb_ref[:, ls], (rows, LANES))
        for t0 in range(0, tm, rows):
            acc = bias
            for k in range(kc):
                q8, r = (off + k) // SUBLANES * SUBLANES, (off + k) % SUBLANES
                if r == 0:
                    tap = win_ref[t0 + q8:t0 + q8 + rows, ls]
                else:
                    tap = sh_ref[r - 1, t0 + q8:t0 + q8 + rows, :]
                acc = acc + tap * w_rows[k]
            y_ref[t0:t0 + rows, ls] = acc

    y = y_ref[...]
    mu = jnp.mean(y, axis=-1, keepdims=True)
    yc = y - mu
    yn = yc * lax.rsqrt(jnp.mean(yc * yc, axis=-1, keepdims=True) + EPS)
    yn = yn * lng_ref[...] + lnb_ref[...]
    act = (yn * jax.nn.sigmoid(yn)).astype(BF16)
    y_conv = jnp.dot(act, wco_ref[...], preferred_element_type=F32)
    y_attn = jnp.dot(att_ref[0], wao_ref[...], preferred_element_type=F32)
    merged = sgc_ref[0].astype(F32) * y_conv + sga_ref[0].astype(F32) * y_attn
    o_ref[0] = x_ref[0] + jnp.dot(merged.astype(BF16), wo_ref[...],
                                  preferred_element_type=F32)


def _conv_merge(a, dw_w, dw_b, ln_g, ln_b, w_co, att, w_ao, sgc, sga, w_o, x, *, tm):
    b, s, d = x.shape
    kc = dw_w.shape[0]
    halo = 32
    assert kc - 1 <= halo and tm % halo == 0
    off = halo - (kc - 1)
    span = tm + max((off + k) // SUBLANES * SUBLANES
                    for k in range(kc) if (off + k) % SUBLANES)
    assert SUBLANES - 1 + span <= tm + halo
    row = pl.BlockSpec((1, tm, d), lambda bi, i: (bi, i, 0))
    halo_spec = pl.BlockSpec(
        (1, halo, d), lambda bi, i: (bi, jnp.maximum(i * (tm // halo) - 1, 0), 0))
    vec = _const_spec((1, d))
    return pl.pallas_call(
        functools.partial(_conv_merge_kernel, tm=tm, kc=kc, halo=halo),
        grid=(b, s // tm),
        in_specs=[row, halo_spec, _const_spec(dw_w.shape), vec, vec, vec,
                  _const_spec(w_co.shape), row, _const_spec(w_ao.shape), row, row,
                  _const_spec(w_o.shape), row],
        out_specs=row,
        out_shape=jax.ShapeDtypeStruct((b, s, d), F32),
        scratch_shapes=[pltpu.VMEM((tm + halo, d), F32),
                        pltpu.VMEM((SUBLANES - 1, span, LANES), F32),
                        pltpu.VMEM((tm, d), F32)],
        compiler_params=pltpu.CompilerParams(
            dimension_semantics=("arbitrary", "arbitrary"),
            vmem_limit_bytes=VMEM_LIMIT),
        name="conv_merge",
    )(a, a, dw_w, dw_b, ln_g, ln_b, w_co, att, w_ao, sgc, sga, w_o, x)


def _ffn_kernel(h_ref, g2_ref, wg_ref, wu_ref, wd_ref, o_ref, *, chunks):
    h = h_ref[...]
    n = h * lax.rsqrt(jnp.mean(h * h, axis=-1, keepdims=True) + EPS) * g2_ref[...]
    n = n.astype(BF16)
    out = h
    for lo, hi in chunks:
        g = jnp.dot(n, wg_ref[:, lo:hi], preferred_element_type=F32)
        u = jnp.dot(n, wu_ref[:, lo:hi], preferred_element_type=F32)
        act = (g * jax.nn.sigmoid(g) * u).astype(BF16)
        out = out + jnp.dot(act, wd_ref[lo:hi, :], preferred_element_type=F32)
    o_ref[...] = out


def _ffn(h2, g2, w_g, w_u, w_d, *, tm):
    t, d = h2.shape
    dff = w_g.shape[1]
    step = 768
    chunks = tuple((lo, min(lo + step, dff)) for lo in range(0, dff, step))
    row = pl.BlockSpec((tm, d), lambda i: (i, 0))
    return pl.pallas_call(
        functools.partial(_ffn_kernel, chunks=chunks),
        grid=(t // tm,),
        in_specs=[row, _const_spec((1, d)), _const_spec(w_g.shape),
                  _const_spec(w_u.shape), _const_spec(w_d.shape)],
        out_specs=row,
        out_shape=jax.ShapeDtypeStruct((t, d), F32),
        compiler_params=pltpu.CompilerParams(
            dimension_semantics=("arbitrary",), vmem_limit_bytes=VMEM_LIMIT),
        name="ffn",
    )(h2, g2, w_g, w_u, w_d)


def _layer(x, norm1_g, w_in, dw_w, dw_b, conv_ln_g, conv_ln_b, w_conv_out,
           q_norm_g, k_norm_g, w_attn_out, w_out, norm2_g,
           w_ffn_gate, w_ffn_up, w_ffn_down):
    b, s, d = x.shape
    t = b * s
    tm = 512
    row = lambda p: p.reshape(1, -1)
    bf = lambda w: w.astype(BF16)

    a, q, k, v, sgc, sga, kmean = _in_proj(
        x.reshape(t, d), row(norm1_g), bf(w_in), row(q_norm_g), row(k_norm_g), tm=tm)
    r3 = lambda z: z.reshape(b, s, d)
    kmean = kmean.reshape(b, s // MOBA_BLOCK, d)
    qt, kf, vt = _attn_prep(r3(q), r3(k), r3(v), kmean, tp=ATTN_TILE)
    att = _moba_attn(qt, kf, vt, s=s, d=d, tq=ATTN_TILE)
    h1 = _conv_merge(r3(a), dw_w, row(dw_b), row(conv_ln_g), row(conv_ln_b),
                     bf(w_conv_out), att, bf(w_attn_out), r3(sgc), r3(sga),
                     bf(w_out), x, tm=tm)
    out = _ffn(h1.reshape(t, d), row(norm2_g), bf(w_ffn_gate), bf(w_ffn_up),
               bf(w_ffn_down), tm=tm)
    return out.reshape(b, s, d)


def kernel(x, norm1_g, w_in, dw_w, dw_b, conv_ln_g, conv_ln_b, w_conv_out,
           q_norm_g, k_norm_g, w_attn_out, w_out, norm2_g,
           w_ffn_gate, w_ffn_up, w_ffn_down):
    h = x
    for l in range(norm1_g.shape[0]):
        h = _layer(h, norm1_g[l], w_in[l], dw_w[l], dw_b[l], conv_ln_g[l],
                   conv_ln_b[l], w_conv_out[l], q_norm_g[l], k_norm_g[l],
                   w_attn_out[l], w_out[l], norm2_g[l],
                   w_ffn_gate[l], w_ffn_up[l], w_ffn_down[l])
    return h
```

```python
import functools
import math

import jax
import jax.numpy as jnp
import ml_dtypes
import numpy as np
from jax import lax
from jax.experimental import pallas as pl
from jax.experimental.pallas import tpu as pltpu

F32 = jnp.float32
BF16 = jnp.bfloat16

N_HEADS = 8
MOBA_BLOCK = 256
MOBA_TOPK = 3
EPS = 1e-6
LOG2E = math.log2(math.e)
N_PIECES = 3
MASKED = -1e30
M_INIT = -1e29

SUBLANES = 8
LANES = 128
VT_PAD = 2 * SUBLANES
VMEM_LIMIT = 56 * 1024 * 1024
ATTN_TILE = 1024


def _const_spec(shape):
    nd = len(shape)
    return pl.BlockSpec(shape, lambda *_: (0,) * nd, pipeline_mode=pl.Buffered(1))


def _alibi_slope(h):
    return 2.0 ** (-8.0 * (h + 1) / N_HEADS)


def _bf16_pieces(x, n):
    pieces = []
    rest = np.float32(x)
    for _ in range(n):
        p = np.float32(rest.astype(ml_dtypes.bfloat16))
        pieces.append(float(p))
        rest = np.float32(rest - p)
    return pieces


def _dwconv_lanes(win_ref, sh_ref, y_ref, dww_ref, dwb_ref, lc, *, tm, kc, halo):
    off = halo - (kc - 1)
    span = sh_ref.shape[1]
    ls = slice(lc * LANES, (lc + 1) * LANES)
    for r in range(1, SUBLANES):
        sh_ref[r - 1] = win_ref[r:r + span, ls]
    rows = 64
    w_rows = [jnp.broadcast_to(dww_ref[k:k + 1, ls], (rows, LANES)) for k in range(kc)]
    bias = jnp.broadcast_to(dwb_ref[:, ls], (rows, LANES))
    for t0 in range(0, tm, rows):
        acc = bias
        for k in range(kc):
            q8, r = (off + k) // SUBLANES * SUBLANES, (off + k) % SUBLANES
            if r == 0:
                tap = win_ref[t0 + q8:t0 + q8 + rows, ls]
            else:
                tap = sh_ref[r - 1, t0 + q8:t0 + q8 + rows, :]
            acc = acc + tap * w_rows[k]
        y_ref[t0:t0 + rows, ls] = acc


def _in_proj_kernel(x_ref, g1_ref, w_ref, qg_ref, kg_ref, dww_ref, dwb_ref, lng_ref,
                    lnb_ref, act_ref, q_ref, k_ref, v_ref, sgc_ref, sga_ref, kmean_ref,
                    win_ref, sh_ref, y_ref, *, d, dh, scale, tiles_per_seq, kc, halo):
    tm = x_ref.shape[0]
    first = pl.program_id(0) % tiles_per_seq == 0

    @pl.when(first)
    def _():
        win_ref[0:halo] = jnp.zeros((halo, d), F32)

    @pl.when(jnp.logical_not(first))
    def _():
        win_ref[0:halo] = win_ref[tm:tm + halo]

    x = x_ref[...]
    n = x * lax.rsqrt(jnp.mean(x * x, axis=-1, keepdims=True) + EPS) * g1_ref[...]
    n = n.astype(BF16)

    def proj(c):
        return jnp.dot(n, w_ref[:, c * d:(c + 1) * d], preferred_element_type=F32)

    cu = proj(0)
    cg = proj(1)
    win_ref[halo:halo + tm] = cu * jax.nn.sigmoid(cg)
    for lc in range(d // LANES):
        _dwconv_lanes(win_ref, sh_ref, y_ref, dww_ref, dwb_ref, lc,
                      tm=tm, kc=kc, halo=halo)

    q = proj(2)
    for h in range(N_HEADS):
        qh = q[:, h * dh:(h + 1) * dh]
        qh = qh * lax.rsqrt(jnp.mean(qh * qh, axis=-1, keepdims=True) + EPS) * qg_ref[...]
        q_ref[:, h * dh:(h + 1) * dh] = (qh * (scale * LOG2E)).astype(BF16)
    k = proj(3)
    for h in range(N_HEADS):
        kh = k[:, h * dh:(h + 1) * dh]
        kh = kh * lax.rsqrt(jnp.mean(kh * kh, axis=-1, keepdims=True) + EPS) * kg_ref[...]
        k_ref[:, h * dh:(h + 1) * dh] = kh.astype(BF16)
        kmean_ref[0, :, h * dh:(h + 1) * dh] = jnp.mean(
            kh.reshape(tm // MOBA_BLOCK, MOBA_BLOCK, dh), axis=1)
    v_ref[...] = proj(4).astype(BF16)
    sgc_ref[...] = jax.nn.sigmoid(proj(5)).astype(BF16)
    sga_ref[...] = jax.nn.sigmoid(proj(6)).astype(BF16)

    y = y_ref[...]
    mu = jnp.mean(y, axis=-1, keepdims=True)
    yc = y - mu
    yn = yc * lax.rsqrt(jnp.mean(yc * yc, axis=-1, keepdims=True) + EPS)
    yn = yn * lng_ref[...] + lnb_ref[...]
    act_ref[...] = (yn * jax.nn.sigmoid(yn)).astype(BF16)


def _in_proj(x2, g1, w_in, qg, kg, dw_w, dw_b, ln_g, ln_b, *, tm, seq):
    t, d = x2.shape
    dh = d // N_HEADS
    nbt = tm // MOBA_BLOCK
    kc = dw_w.shape[0]
    halo = 32
    assert kc - 1 <= halo and halo % SUBLANES == 0 and seq % tm == 0
    off = halo - (kc - 1)
    span = tm + max((off + k) // SUBLANES * SUBLANES
                    for k in range(kc) if (off + k) % SUBLANES)
    assert SUBLANES - 1 + span <= tm + halo
    nt = t // tm
    row = pl.BlockSpec((tm, d), lambda i: (i, 0))
    vec = _const_spec((1, d))
    out_bf = jax.ShapeDtypeStruct((t, d), BF16)
    return pl.pallas_call(
        functools.partial(_in_proj_kernel, d=d, dh=dh, scale=dh ** -0.5,
                          tiles_per_seq=seq // tm, kc=kc, halo=halo),
        grid=(nt,),
        in_specs=[row, vec, _const_spec(w_in.shape),
                  _const_spec((1, dh)), _const_spec((1, dh)),
                  _const_spec(dw_w.shape), vec, vec, vec],
        out_specs=[row, row, row, row, row, row,
                   pl.BlockSpec((1, nbt, d), lambda i: (i, 0, 0))],
        out_shape=[out_bf] * 6 + [jax.ShapeDtypeStruct((nt, nbt, d), F32)],
        scratch_shapes=[pltpu.VMEM((tm + halo, d), F32),
                        pltpu.VMEM((SUBLANES - 1, span, LANES), F32),
                        pltpu.VMEM((tm, d), F32)],
        compiler_params=pltpu.CompilerParams(
            dimension_semantics=("arbitrary",), vmem_limit_bytes=VMEM_LIMIT),
        name="in_proj",
    )(x2, g1, w_in, qg, kg, dw_w, dw_b, ln_g, ln_b)


def _attn_prep_kernel(q_ref, k_ref, v_ref, kmean_ref, qt_ref, kf_ref, vt_ref,
                      *, dh, nb, tp):
    i = pl.program_id(1)
    base = i * tp
    nx = dh

    eye = (lax.broadcasted_iota(jnp.int32, (dh, dh), 0)
           == lax.broadcasted_iota(jnp.int32, (dh, dh), 1)).astype(BF16)
    nt = (((1,), (1,)), ((), ()))
    ones_rows = (lax.broadcasted_iota(jnp.int32, (VT_PAD, tp), 0) == 0).astype(BF16)

    blk = lax.broadcasted_iota(jnp.int32, (nb, tp), 0)
    qpos = base + lax.broadcasted_iota(jnp.int32, (nb, tp), 1)
    cur = qpos // MOBA_BLOCK
    past = blk < cur
    xrow = lax.broadcasted_iota(jnp.int32, (nx - nb, tp), 0)

    kcol = lax.broadcasted_iota(jnp.int32, (tp, nx), 1)
    kpos = base + lax.broadcasted_iota(jnp.int32, (tp, nx), 0)
    kblk = kpos // MOBA_BLOCK
    kpiece = kcol - nb
    is_pos = (kpiece >= 0) & (kpiece < 2 * N_PIECES)
    kx = jnp.where(kcol == kblk, 1.0, 0.0)
    kx = jnp.where(is_pos & (kpiece % 2 == 0), (MOBA_BLOCK * kblk).astype(F32), kx)
    kx = jnp.where(is_pos & (kpiece % 2 == 1), (kpos % MOBA_BLOCK).astype(F32), kx)
    kx = kx.astype(BF16)

    for h in range(N_HEADS):
        hs = slice(h * dh, (h + 1) * dh)
        qh = q_ref[0, :, hs]
        km = kmean_ref[0, :, hs]
        km_hi = km.astype(BF16)
        r1 = km - km_hi.astype(F32)
        km_mid = r1.astype(BF16)
        km_lo = (r1 - km_mid.astype(F32)).astype(BF16)
        lhs = jnp.concatenate([km_hi, km_mid, km_lo, eye], axis=0)
        r = lax.dot_general(lhs, qh, nt, preferred_element_type=F32)
        gate = r[0:nb] + r[nb:2 * nb] + r[2 * nb:3 * nb]
        q_t = r[3 * nb:3 * nb + dh]

        g = jnp.where(past, gate, -jnp.inf)
        sel = jnp.zeros((nb, tp), jnp.bool_)
        for _ in range(min(MOBA_TOPK, nb)):
            mx = jnp.max(g, axis=0, keepdims=True)
            first = jnp.min(jnp.where(g == mx, blk, nb), axis=0, keepdims=True)
            pick = blk == first
            sel = jnp.logical_or(sel, pick)
            g = jnp.where(pick, -jnp.inf, g)
        keep = jnp.logical_or(jnp.logical_and(sel, past), blk == cur)
        selbias = jnp.where(keep, 0.0, MASKED).astype(F32)

        xq = jnp.zeros((nx - nb, tp), F32)
        for t, c_t in enumerate(_bf16_pieces(_alibi_slope(h) * LOG2E, N_PIECES)):
            xq = jnp.where(xrow // 2 == t, c_t, xq)
        qt_ref[0, h] = jnp.concatenate([q_t, selbias, xq], axis=0).astype(BF16)
        kh = k_ref[0, :, hs]
        v_t = lax.dot_general(eye, v_ref[0, :, hs], nt,
                              preferred_element_type=F32).astype(BF16)
        v_t = jnp.concatenate([v_t, ones_rows], axis=0)
        for s in range(tp // MOBA_BLOCK):
            rs = slice(s * MOBA_BLOCK, (s + 1) * MOBA_BLOCK)
            kf_ref[0, h, s, :, 0:dh] = kh[rs]
            kf_ref[0, h, s, :, dh:2 * dh] = kx[rs]
            vt_ref[0, h, s] = v_t[:, rs]


def _attn_prep(q, k, v, kmean, *, tp):
    b, s, d = q.shape
    dh = d // N_HEADS
    nb = s // MOBA_BLOCK
    nsub = tp // MOBA_BLOCK
    assert dh + nb + 2 * N_PIECES <= 2 * dh
    row = pl.BlockSpec((1, tp, d), lambda bi, i: (bi, i, 0))
    return pl.pallas_call(
        functools.partial(_attn_prep_kernel, dh=dh, nb=nb, tp=tp),
        grid=(b, s // tp),
        in_specs=[row, row, row, pl.BlockSpec((1, nb, d), lambda bi, i: (bi, 0, 0))],
        out_specs=[
            pl.BlockSpec((1, N_HEADS, 2 * dh, tp), lambda bi, i: (bi, 0, 0, i)),
            pl.BlockSpec((1, N_HEADS, nsub, MOBA_BLOCK, 2 * dh),
                         lambda bi, i: (bi, 0, i, 0, 0)),
            pl.BlockSpec((1, N_HEADS, nsub, dh + VT_PAD, MOBA_BLOCK),
                         lambda bi, i: (bi, 0, i, 0, 0)),
        ],
        out_shape=[
            jax.ShapeDtypeStruct((b, N_HEADS, 2 * dh, s), BF16),
            jax.ShapeDtypeStruct((b, N_HEADS, nb, MOBA_BLOCK, 2 * dh), BF16),
            jax.ShapeDtypeStruct((b, N_HEADS, nb, dh + VT_PAD, MOBA_BLOCK), BF16),
        ],
        compiler_params=pltpu.CompilerParams(
            dimension_semantics=("arbitrary", "arbitrary"),
            vmem_limit_bytes=VMEM_LIMIT),
        name="attn_prep",
    )(q, k, v, kmean)


def _moba_attn_kernel(qt_ref, kf_ref, vt_ref, o_ref, s0_ref, s1_ref, s2_ref, s3_ref,
                      m_ref, acc_ref, *, tq):
    i = pl.program_id(2)
    blk = MOBA_BLOCK
    ngrp = tq // blk
    npast = i * ngrp
    m_ref[...] = jnp.full(m_ref.shape, M_INIT, F32)
    acc_ref[...] = jnp.zeros(acc_ref.shape, F32)

    s_refs = (s0_ref, s1_ref, s2_ref, s3_ref)

    def scores(j, c, buf):
        s_refs[buf][c] = jnp.dot(kf_ref[0, 0, j], qt_ref[0, 0, :, c * blk:(c + 1) * blk],
                                 preferred_element_type=F32)

    def absorb(c, items):
        cs = slice(c * blk, (c + 1) * blk)
        tiles = []
        for j, buf, causal in items:
            s = s_refs[buf][c]
            if causal:
                kpos = lax.broadcasted_iota(jnp.int32, s.shape, 0)
                qpos = lax.broadcasted_iota(jnp.int32, s.shape, 1)
                s = jnp.where(kpos <= qpos, s, MASKED)
            tiles.append(s)
        m_old = m_ref[:, cs]
        m_new = m_old
        for s in tiles:
            m_new = jnp.maximum(m_new, jnp.max(s, axis=0, keepdims=True))
        alpha = jnp.exp2(m_old - m_new)
        pv = None
        for (j, _, _), s in zip(items, tiles):
            p = jnp.exp2(s - m_new).astype(BF16)
            pj = jnp.dot(vt_ref[0, 0, j], p, preferred_element_type=F32)
            pv = pj if pv is None else pv + pj
        acc_ref[:, cs] = alpha * acc_ref[:, cs] + pv
        m_ref[:, cs] = m_new

    for c in range(ngrp):
        scores(0, c, 0)
        scores(1, c, 1)

    def past_quad(t, carry):
        for half in range(2):
            j = 4 * t + 2 * half
            cur, nxt = 2 * half, 2 * (1 - half)
            for c in range(ngrp):
                scores(j + 2, c, nxt)
                scores(j + 3, c, nxt + 1)
                absorb(c, [(j, cur, False), (j + 1, cur + 1, False)])
        return carry

    lax.fori_loop(0, npast // 4, past_quad, 0)

    for c in range(2, ngrp):
        for g in range(2, c + 1):
            scores(npast + g, c, g)
    for c in range(ngrp):
        own = [(npast + g, g, g == c) for g in range(c + 1)]
        for lo in range(0, len(own), 2):
            absorb(c, own[lo:lo + 2])
    dh = acc_ref.shape[0] - VT_PAD
    o_ref[0] = (acc_ref[0:dh, :] / acc_ref[dh:dh + 1, :]).T.astype(o_ref.dtype)


def _moba_attn(qt, kf, vt, *, s, d, tq):
    b, nh, dq, _ = qt.shape
    nb = kf.shape[2]
    dh = d // nh
    blk = MOBA_BLOCK
    assert tq // blk == 4
    sbuf = pltpu.VMEM((tq // blk, blk, blk), F32)
    return pl.pallas_call(
        functools.partial(_moba_attn_kernel, tq=tq),
        grid=(b, nh, s // tq),
        in_specs=[
            pl.BlockSpec((1, 1, dq, tq), lambda bi, h, i: (bi, h, 0, i)),
            pl.BlockSpec((1, 1, nb, blk, dq), lambda bi, h, i: (bi, h, 0, 0, 0)),
            pl.BlockSpec((1, 1, nb, dh + VT_PAD, blk), lambda bi, h, i: (bi, h, 0, 0, 0)),
        ],
        out_specs=pl.BlockSpec((1, tq, dh), lambda bi, h, i: (bi, i, h)),
        out_shape=jax.ShapeDtypeStruct((b, s, d), BF16),
        scratch_shapes=[sbuf, sbuf, sbuf, sbuf,
                        pltpu.VMEM((1, tq), F32), pltpu.VMEM((dh + VT_PAD, tq), F32)],
        compiler_params=pltpu.CompilerParams(
            dimension_semantics=("arbitrary", "arbitrary", "arbitrary"),
            vmem_limit_bytes=VMEM_LIMIT),
        name="moba_attn",
    )(qt, kf, vt)


def _merge_ffn_kernel(act_ref, att_ref, sgc_ref, sga_ref, x_ref, wco_ref, wao_ref,
                      wo_ref, g2_ref, wg_ref, wu_ref, wd_ref, o_ref, *, chunks):
    y_conv = jnp.dot(act_ref[...], wco_ref[...], preferred_element_type=F32)
    y_attn = jnp.dot(att_ref[...], wao_ref[...], preferred_element_type=F32)
    merged = sgc_ref[...].astype(F32) * y_conv + sga_ref[...].astype(F32) * y_attn
    h = x_ref[...] + jnp.dot(merged.astype(BF16), wo_ref[...],
                             preferred_element_type=F32)
    n = h * lax.rsqrt(jnp.mean(h * h, axis=-1, keepdims=True) + EPS) * g2_ref[...]
    n = n.astype(BF16)
    out = h
    for lo, hi in chunks:
        g = jnp.dot(n, wg_ref[:, lo:hi], preferred_element_type=F32)
        u = jnp.dot(n, wu_ref[:, lo:hi], preferred_element_type=F32)
        a = (g * jax.nn.sigmoid(g) * u).astype(BF16)
        out = out + jnp.dot(a, wd_ref[lo:hi, :], preferred_element_type=F32)
    o_ref[...] = out


def _merge_ffn(act, att, sgc, sga, x2, w_co, w_ao, w_o, g2, w_g, w_u, w_d, *, tm):
    t, d = x2.shape
    dff = w_g.shape[1]
    step = 768
    chunks = tuple((lo, min(lo + step, dff)) for lo in range(0, dff, step))
    row = pl.BlockSpec((tm, d), lambda i: (i, 0))
    return pl.pallas_call(
        functools.partial(_merge_ffn_kernel, chunks=chunks),
        grid=(t // tm,),
        in_specs=[row, row, row, row, row, _const_spec(w_co.shape),
                  _const_spec(w_ao.shape), _const_spec(w_o.shape), _const_spec((1, d)),
                  _const_spec(w_g.shape), _const_spec(w_u.shape), _const_spec(w_d.shape)],
        out_specs=row,
        out_shape=jax.ShapeDtypeStruct((t, d), F32),
        compiler_params=pltpu.CompilerParams(
            dimension_semantics=("arbitrary",), vmem_limit_bytes=VMEM_LIMIT),
        name="merge_ffn",
    )(act, att, sgc, sga, x2, w_co, w_ao, w_o, g2, w_g, w_u, w_d)


def _layer(x, norm1_g, w_in, dw_w, dw_b, conv_ln_g, conv_ln_b, w_conv_out,
           q_norm_g, k_norm_g, w_attn_out, w_out, norm2_g,
           w_ffn_gate, w_ffn_up, w_ffn_down):
    b, s, d = x.shape
    t = b * s
    tm = 512
    row = lambda p: p.reshape(1, -1)
    bf = lambda w: w.astype(BF16)

    x2 = x.reshape(t, d)
    act, q, k, v, sgc, sga, kmean = _in_proj(
        x2, row(norm1_g), bf(w_in), row(q_norm_g), row(k_norm_g),
        dw_w, row(dw_b), row(conv_ln_g), row(conv_ln_b), tm=tm, seq=s)
    r3 = lambda z: z.reshape(b, s, d)
    kmean = kmean.reshape(b, s // MOBA_BLOCK, d)
    qt, kf, vt = _attn_prep(r3(q), r3(k), r3(v), kmean, tp=ATTN_TILE)
    att = _moba_attn(qt, kf, vt, s=s, d=d, tq=ATTN_TILE)
    out = _merge_ffn(act, att.reshape(t, d), sgc, sga, x2, bf(w_conv_out),
                     bf(w_attn_out), bf(w_out), row(norm2_g), bf(w_ffn_gate),
                     bf(w_ffn_up), bf(w_ffn_down), tm=tm)
    return out.reshape(b, s, d)


def kernel(x, norm1_g, w_in, dw_w, dw_b, conv_ln_g, conv_ln_b, w_conv_out,
           q_norm_g, k_norm_g, w_attn_out, w_out, norm2_g,
           w_ffn_gate, w_ffn_up, w_ffn_down):
    h = x
    for l in range(norm1_g.shape[0]):
        h = _layer(h, norm1_g[l], w_in[l], dw_w[l], dw_b[l], conv_ln_g[l],
                   conv_ln_b[l], w_conv_out[l], q_norm_g[l], k_norm_g[l],
                   w_attn_out[l], w_out[l], norm2_g[l],
                   w_ffn_gate[l], w_ffn_up[l], w_ffn_down[l])
    return h
```

```python
import functools
import math

import jax
import jax.numpy as jnp
import ml_dtypes
import numpy as np
from jax import lax
from jax.experimental import pallas as pl
from jax.experimental.pallas import tpu as pltpu

F32 = jnp.float32
BF16 = jnp.bfloat16

N_HEADS = 8
MOBA_BLOCK = 256
MOBA_TOPK = 3
EPS = 1e-6
LOG2E = math.log2(math.e)
N_PIECES = 3
MASKED = -1e30
M_INIT = -1e29

SUBLANES = 8
LANES = 128
VT_PAD = 2 * SUBLANES
VMEM_LIMIT = 56 * 1024 * 1024
ATTN_TILE = 1024


def _const_spec(shape):
    nd = len(shape)
    return pl.BlockSpec(shape, lambda *_: (0,) * nd, pipeline_mode=pl.Buffered(1))


def _alibi_slope(h):
    return 2.0 ** (-8.0 * (h + 1) / N_HEADS)


def _bf16_pieces(x, n):
    pieces = []
    rest = np.float32(x)
    for _ in range(n):
        p = np.float32(rest.astype(ml_dtypes.bfloat16))
        pieces.append(float(p))
        rest = np.float32(rest - p)
    return pieces


def _dwconv_lanes(win_ref, sh_ref, y_ref, dww_ref, dwb_ref, lc, *, tm, kc, halo):
    off = halo - (kc - 1)
    span = sh_ref.shape[1]
    ls = slice(lc * LANES, (lc + 1) * LANES)
    for r in range(1, SUBLANES):
        sh_ref[r - 1] = win_ref[r:r + span, ls]
    rows = 64
    w_rows = [jnp.broadcast_to(dww_ref[k:k + 1, ls], (rows, LANES)) for k in range(kc)]
    bias = jnp.broadcast_to(dwb_ref[:, ls], (rows, LANES))
    for t0 in range(0, tm, rows):
        acc = bias
        for k in range(kc):
            q8, r = (off + k) // SUBLANES * SUBLANES, (off + k) % SUBLANES
            if r == 0:
                tap = win_ref[t0 + q8:t0 + q8 + rows, ls]
            else:
                tap = sh_ref[r - 1, t0 + q8:t0 + q8 + rows, :]
            acc = acc + tap * w_rows[k]
        y_ref[t0:t0 + rows, ls] = acc


def _in_proj_kernel(x_ref, g1_ref, w_ref, qg_ref, kg_ref, dww_ref, dwb_ref, lng_ref,
                    lnb_ref, act_ref, q_ref, k_ref, v_ref, sgc_ref, sga_ref, kmean_ref,
                    win_ref, sh_ref, y_ref, *, d, dh, scale, tiles_per_seq, kc, halo):
    tm = x_ref.shape[0]
    first = pl.program_id(0) % tiles_per_seq == 0

    @pl.when(first)
    def _():
        win_ref[0:halo] = jnp.zeros((halo, d), F32)

    @pl.when(jnp.logical_not(first))
    def _():
        win_ref[0:halo] = win_ref[tm:tm + halo]

    x = x_ref[...]
    n = x * lax.rsqrt(jnp.mean(x * x, axis=-1, keepdims=True) + EPS) * g1_ref[...]
    n = n.astype(BF16)

    def proj(c):
        return jnp.dot(n, w_ref[:, c * d:(c + 1) * d], preferred_element_type=F32)

    cu = proj(0)
    cg = proj(1)
    win_ref[halo:halo + tm] = cu * jax.nn.sigmoid(cg)
    for lc in range(d // LANES):
        _dwconv_lanes(win_ref, sh_ref, y_ref, dww_ref, dwb_ref, lc,
                      tm=tm, kc=kc, halo=halo)

    q = proj(2)
    for h in range(N_HEADS):
        qh = q[:, h * dh:(h + 1) * dh]
        qh = qh * lax.rsqrt(jnp.mean(qh * qh, axis=-1, keepdims=True) + EPS) * qg_ref[...]
        q_ref[:, h * dh:(h + 1) * dh] = (qh * (scale * LOG2E)).astype(BF16)
    k = proj(3)
    for h in range(N_HEADS):
        kh = k[:, h * dh:(h + 1) * dh]
        kh = kh * lax.rsqrt(jnp.mean(kh * kh, axis=-1, keepdims=True) + EPS) * kg_ref[...]
        k_ref[:, h * dh:(h + 1) * dh] = kh.astype(BF16)
        kmean_ref[0, :, h * dh:(h + 1) * dh] = jnp.mean(
            kh.reshape(tm // MOBA_BLOCK, MOBA_BLOCK, dh), axis=1)
    v_ref[...] = proj(4).astype(BF16)
    sgc_ref[...] = jax.nn.sigmoid(proj(5)).astype(BF16)
    sga_ref[...] = jax.nn.sigmoid(proj(6)).astype(BF16)

    y = y_ref[...]
    mu = jnp.mean(y, axis=-1, keepdims=True)
    yc = y - mu
    yn = yc * lax.rsqrt(jnp.mean(yc * yc, axis=-1, keepdims=True) + EPS)
    yn = yn * lng_ref[...] + lnb_ref[...]
    act_ref[...] = (yn * jax.nn.sigmoid(yn)).astype(BF16)


def _in_proj(x2, g1, w_in, qg, kg, dw_w, dw_b, ln_g, ln_b, *, tm, seq):
    t, d = x2.shape
    dh = d // N_HEADS
    nbt = tm // MOBA_BLOCK
    kc = dw_w.shape[0]
    halo = 32
    assert kc - 1 <= halo and halo % SUBLANES == 0 and seq % tm == 0
    off = halo - (kc - 1)
    span = tm + max((off + k) // SUBLANES * SUBLANES
                    for k in range(kc) if (off + k) % SUBLANES)
    assert SUBLANES - 1 + span <= tm + halo
    nt = t // tm
    row = pl.BlockSpec((tm, d), lambda i: (i, 0))
    vec = _const_spec((1, d))
    out_bf = jax.ShapeDtypeStruct((t, d), BF16)
    return pl.pallas_call(
        functools.partial(_in_proj_kernel, d=d, dh=dh, scale=dh ** -0.5,
                          tiles_per_seq=seq // tm, kc=kc, halo=halo),
        grid=(nt,),
        in_specs=[row, vec, _const_spec(w_in.shape),
                  _const_spec((1, dh)), _const_spec((1, dh)),
                  _const_spec(dw_w.shape), vec, vec, vec],
        out_specs=[row, row, row, row, row, row,
                   pl.BlockSpec((1, nbt, d), lambda i: (i, 0, 0))],
        out_shape=[out_bf] * 6 + [jax.ShapeDtypeStruct((nt, nbt, d), F32)],
        scratch_shapes=[pltpu.VMEM((tm + halo, d), F32),
                        pltpu.VMEM((SUBLANES - 1, span, LANES), F32),
                        pltpu.VMEM((tm, d), F32)],
        compiler_params=pltpu.CompilerParams(
            dimension_semantics=("arbitrary",), vmem_limit_bytes=VMEM_LIMIT),
        name="in_proj",
    )(x2, g1, w_in, qg, kg, dw_w, dw_b, ln_g, ln_b)


def _attn_prep_kernel(q_ref, k_ref, v_ref, kmean_ref, qt_ref, kf_ref, vt_ref,
                      *, dh, nb, tp):
    i = pl.program_id(1)
    base = i * tp
    nx = dh

    eye = (lax.broadcasted_iota(jnp.int32, (dh, dh), 0)
           == lax.broadcasted_iota(jnp.int32, (dh, dh), 1)).astype(BF16)
    nt = (((1,), (1,)), ((), ()))
    ones_rows = (lax.broadcasted_iota(jnp.int32, (VT_PAD, tp), 0) == 0).astype(BF16)

    blk = lax.broadcasted_iota(jnp.int32, (nb, tp), 0)
    qpos = base + lax.broadcasted_iota(jnp.int32, (nb, tp), 1)
    cur = qpos // MOBA_BLOCK
    past = blk < cur
    xrow = lax.broadcasted_iota(jnp.int32, (nx - nb, tp), 0)

    kcol = lax.broadcasted_iota(jnp.int32, (tp, nx), 1)
    kpos = base + lax.broadcasted_iota(jnp.int32, (tp, nx), 0)
    kblk = kpos // MOBA_BLOCK
    kpiece = kcol - nb
    is_pos = (kpiece >= 0) & (kpiece < 2 * N_PIECES)
    kx = jnp.where(kcol == kblk, 1.0, 0.0)
    kx = jnp.where(is_pos & (kpiece % 2 == 0), (MOBA_BLOCK * kblk).astype(F32), kx)
    kx = jnp.where(is_pos & (kpiece % 2 == 1), (kpos % MOBA_BLOCK).astype(F32), kx)
    kx = kx.astype(BF16)

    for h in range(N_HEADS):
        hs = slice(h * dh, (h + 1) * dh)
        qh = q_ref[0, :, hs]
        km = kmean_ref[0, :, hs]
        km_hi = km.astype(BF16)
        r1 = km - km_hi.astype(F32)
        km_mid = r1.astype(BF16)
        km_lo = (r1 - km_mid.astype(F32)).astype(BF16)
        lhs = jnp.concatenate([km_hi, km_mid, km_lo, eye], axis=0)
        r = lax.dot_general(lhs, qh, nt, preferred_element_type=F32)
        gate = r[0:nb] + r[nb:2 * nb] + r[2 * nb:3 * nb]
        q_t = r[3 * nb:3 * nb + dh]

        g = jnp.where(past, gate, -jnp.inf)
        sel = jnp.zeros((nb, tp), jnp.bool_)
        for _ in range(min(MOBA_TOPK, nb)):
            mx = jnp.max(g, axis=0, keepdims=True)
            first = jnp.min(jnp.where(g == mx, blk, nb), axis=0, keepdims=True)
            pick = blk == first
            sel = jnp.logical_or(sel, pick)
            g = jnp.where(pick, -jnp.inf, g)
        keep = jnp.logical_or(jnp.logical_and(sel, past), blk == cur)
        selbias = jnp.where(keep, 0.0, MASKED).astype(F32)

        xq = jnp.zeros((nx - nb, tp), F32)
        for t, c_t in enumerate(_bf16_pieces(_alibi_slope(h) * LOG2E, N_PIECES)):
            xq = jnp.where(xrow // 2 == t, c_t, xq)
        qt_ref[0, h] = jnp.concatenate([q_t, selbias, xq], axis=0).astype(BF16)
        kh = k_ref[0, :, hs]
        v_t = lax.dot_general(eye, v_ref[0, :, hs], nt,
                              preferred_element_type=F32).astype(BF16)
        v_t = jnp.concatenate([v_t, ones_rows], axis=0)
        for s in range(tp // MOBA_BLOCK):
            rs = slice(s * MOBA_BLOCK, (s + 1) * MOBA_BLOCK)
            kf_ref[0, h, s, :, 0:dh] = kh[rs]
            kf_ref[0, h, s, :, dh:2 * dh] = kx[rs]
            vt_ref[0, h, s] = v_t[:, rs]


def _attn_prep(q, k, v, kmean, *, tp):
    b, s, d = q.shape
    dh = d // N_HEADS
    nb = s // MOBA_BLOCK
    nsub = tp // MOBA_BLOCK
    assert dh + nb + 2 * N_PIECES <= 2 * dh
    row = pl.BlockSpec((1, tp, d), lambda bi, i: (bi, i, 0))
    return pl.pallas_call(
        functools.partial(_attn_prep_kernel, dh=dh, nb=nb, tp=tp),
        grid=(b, s // tp),
        in_specs=[row, row, row, pl.BlockSpec((1, nb, d), lambda bi, i: (bi, 0, 0))],
        out_specs=[
            pl.BlockSpec((1, N_HEADS, 2 * dh, tp), lambda bi, i: (bi, 0, 0, i)),
            pl.BlockSpec((1, N_HEADS, nsub, MOBA_BLOCK, 2 * dh),
                         lambda bi, i: (bi, 0, i, 0, 0)),
            pl.BlockSpec((1, N_HEADS, nsub, dh + VT_PAD, MOBA_BLOCK),
                         lambda bi, i: (bi, 0, i, 0, 0)),
        ],
        out_shape=[
            jax.ShapeDtypeStruct((b, N_HEADS, 2 * dh, s), BF16),
            jax.ShapeDtypeStruct((b, N_HEADS, nb, MOBA_BLOCK, 2 * dh), BF16),
            jax.ShapeDtypeStruct((b, N_HEADS, nb, dh + VT_PAD, MOBA_BLOCK), BF16),
        ],
        compiler_params=pltpu.CompilerParams(
            dimension_semantics=("arbitrary", "arbitrary"),
            vmem_limit_bytes=VMEM_LIMIT),
        name="attn_prep",
    )(q, k, v, kmean)


def _moba_attn_kernel(qt_ref, kf_ref, vt_ref, o_ref, s0_ref, s1_ref, s2_ref, s3_ref,
                      m_ref, acc_ref, *, tq):
    i = pl.program_id(2)
    blk = MOBA_BLOCK
    ngrp = tq // blk
    npast = i * ngrp
    m_ref[...] = jnp.full(m_ref.shape, M_INIT, F32)
    acc_ref[...] = jnp.zeros(acc_ref.shape, F32)

    s_refs = (s0_ref, s1_ref, s2_ref, s3_ref)

    def scores(j, c, buf):
        s_refs[buf][c] = jnp.dot(kf_ref[0, 0, j], qt_ref[0, 0, :, c * blk:(c + 1) * blk],
                                 preferred_element_type=F32)

    def probs(c, items):
        cs = slice(c * blk, (c + 1) * blk)
        tiles = []
        for _, buf, causal in items:
            s = s_refs[buf][c]
            if causal:
                kpos = lax.broadcasted_iota(jnp.int32, s.shape, 0)
                qpos = lax.broadcasted_iota(jnp.int32, s.shape, 1)
                s = jnp.where(kpos <= qpos, s, MASKED)
            tiles.append(s)
        m_old = m_ref[:, cs]
        m_new = m_old
        for s in tiles:
            m_new = jnp.maximum(m_new, jnp.max(s, axis=0, keepdims=True))
        m_ref[:, cs] = m_new
        return jnp.exp2(m_old - m_new), [jnp.exp2(s - m_new).astype(BF16) for s in tiles]

    def accumulate(c, alpha, ps, js):
        cs = slice(c * blk, (c + 1) * blk)
        pv = None
        for j, p in zip(js, ps):
            pj = jnp.dot(vt_ref[0, 0, j], p, preferred_element_type=F32)
            pv = pj if pv is None else pv + pj
        acc_ref[:, cs] = alpha * acc_ref[:, cs] + pv

    def absorb(c, items):
        alpha, ps = probs(c, items)
        accumulate(c, alpha, ps, [j for j, _, _ in items])

    for c in range(ngrp):
        scores(0, c, 0)
        scores(1, c, 1)

    def past_quad(t):
        for half in range(2):
            j = 4 * t + 2 * half
            cur, nxt = 2 * half, 2 * (1 - half)
            for c in range(ngrp):
                scores(j + 2, c, nxt)
                scores(j + 3, c, nxt + 1)
                absorb(c, [(j, cur, False), (j + 1, cur + 1, False)])

    def past_octet(t, carry):
        past_quad(2 * t)
        past_quad(2 * t + 1)
        return carry

    nquad = npast // 4
    lax.fori_loop(0, nquad // 2, past_octet, 0)

    @pl.when(nquad % 2 == 1)
    def _():
        past_quad(nquad - 1)

    for c in range(2, ngrp):
        for g in range(2, c + 1):
            scores(npast + g, c, g)
    for c in range(ngrp):
        own = [(npast + g, g, g == c) for g in range(c + 1)]
        for lo in range(0, len(own), 2):
            absorb(c, own[lo:lo + 2])
    dh = acc_ref.shape[0] - VT_PAD
    o_ref[0] = (acc_ref[0:dh, :] / acc_ref[dh:dh + 1, :]).T.astype(o_ref.dtype)


def _moba_attn(qt, kf, vt, *, s, d, tq):
    b, nh, dq, _ = qt.shape
    nb = kf.shape[2]
    dh = d // nh
    blk = MOBA_BLOCK
    assert tq // blk == 4
    sbuf = pltpu.VMEM((tq // blk, blk, blk), F32)
    return pl.pallas_call(
        functools.partial(_moba_attn_kernel, tq=tq),
        grid=(b, nh, s // tq),
        in_specs=[
            pl.BlockSpec((1, 1, dq, tq), lambda bi, h, i: (bi, h, 0, i)),
            pl.BlockSpec((1, 1, nb, blk, dq), lambda bi, h, i: (bi, h, 0, 0, 0)),
            pl.BlockSpec((1, 1, nb, dh + VT_PAD, blk), lambda bi, h, i: (bi, h, 0, 0, 0)),
        ],
        out_specs=pl.BlockSpec((1, tq, dh), lambda bi, h, i: (bi, i, h)),
        out_shape=jax.ShapeDtypeStruct((b, s, d), BF16),
        scratch_shapes=[sbuf, sbuf, sbuf, sbuf,
                        pltpu.VMEM((1, tq), F32), pltpu.VMEM((dh + VT_PAD, tq), F32)],
        compiler_params=pltpu.CompilerParams(
            dimension_semantics=("arbitrary", "arbitrary", "arbitrary"),
            vmem_limit_bytes=VMEM_LIMIT),
        name="moba_attn",
    )(qt, kf, vt)


def _merge_ffn_kernel(act_ref, att_ref, sgc_ref, sga_ref, x_ref, wco_ref, wao_ref,
                      wo_ref, g2_ref, wg_ref, wu_ref, wd_ref, o_ref, *, chunks):
    y_conv = jnp.dot(act_ref[...], wco_ref[...], preferred_element_type=F32)
    y_attn = jnp.dot(att_ref[...], wao_ref[...], preferred_element_type=F32)
    merged = sgc_ref[...].astype(F32) * y_conv + sga_ref[...].astype(F32) * y_attn
    h = x_ref[...] + jnp.dot(merged.astype(BF16), wo_ref[...],
                             preferred_element_type=F32)
    n = h * lax.rsqrt(jnp.mean(h * h, axis=-1, keepdims=True) + EPS) * g2_ref[...]
    n = n.astype(BF16)
    out = h
    for lo, hi in chunks:
        g = jnp.dot(n, wg_ref[:, lo:hi], preferred_element_type=F32)
        u = jnp.dot(n, wu_ref[:, lo:hi], preferred_element_type=F32)
        a = (g * jax.nn.sigmoid(g) * u).astype(BF16)
        out = out + jnp.dot(a, wd_ref[lo:hi, :], preferred_element_type=F32)
    o_ref[...] = out


def _merge_ffn(act, att, sgc, sga, x2, w_co, w_ao, w_o, g2, w_g, w_u, w_d, *, tm):
    t, d = x2.shape
    dff = w_g.shape[1]
    step = 768
    chunks = tuple((lo, min(lo + step, dff)) for lo in range(0, dff, step))
    row = pl.BlockSpec((tm, d), lambda i: (i, 0))
    return pl.pallas_call(
        functools.partial(_merge_ffn_kernel, chunks=chunks),
        grid=(t // tm,),
        in_specs=[row, row, row, row, row, _const_spec(w_co.shape),
                  _const_spec(w_ao.shape), _const_spec(w_o.shape), _const_spec((1, d)),
                  _const_spec(w_g.shape), _const_spec(w_u.shape), _const_spec(w_d.shape)],
        out_specs=row,
        out_shape=jax.ShapeDtypeStruct((t, d), F32),
        compiler_params=pltpu.CompilerParams(
            dimension_semantics=("arbitrary",), vmem_limit_bytes=VMEM_LIMIT),
        name="merge_ffn",
    )(act, att, sgc, sga, x2, w_co, w_ao, w_o, g2, w_g, w_u, w_d)


def _layer(x, norm1_g, w_in, dw_w, dw_b, conv_ln_g, conv_ln_b, w_conv_out,
           q_norm_g, k_norm_g, w_attn_out, w_out, norm2_g,
           w_ffn_gate, w_ffn_up, w_ffn_down):
    b, s, d = x.shape
    t = b * s
    tm = 512
    row = lambda p: p.reshape(1, -1)
    bf = lambda w: w.astype(BF16)

    x2 = x.reshape(t, d)
    act, q, k, v, sgc, sga, kmean = _in_proj(
        x2, row(norm1_g), bf(w_in), row(q_norm_g), row(k_norm_g),
        dw_w, row(dw_b), row(conv_ln_g), row(conv_ln_b), tm=tm, seq=s)
    r3 = lambda z: z.reshape(b, s, d)
    kmean = kmean.reshape(b, s // MOBA_BLOCK, d)
    qt, kf, vt = _attn_prep(r3(q), r3(k), r3(v), kmean, tp=ATTN_TILE)
    att = _moba_attn(qt, kf, vt, s=s, d=d, tq=ATTN_TILE)
    out = _merge_ffn(act, att.reshape(t, d), sgc, sga, x2, bf(w_conv_out),
                     bf(w_attn_out), bf(w_out), row(norm2_g), bf(w_ffn_gate),
                     bf(w_ffn_up), bf(w_ffn_down), tm=tm)
    return out.reshape(b, s, d)


def kernel(x, norm1_g, w_in, dw_w, dw_b, conv_ln_g, conv_ln_b, w_conv_out,
           q_norm_g, k_norm_g, w_attn_out, w_out, norm2_g,
           w_ffn_gate, w_ffn_up, w_ffn_down):
    h = x
    for l in range(norm1_g.shape[0]):
        h = _layer(h, norm1_g[l], w_in[l], dw_w[l], dw_b[l], conv_ln_g[l],
                   conv_ln_b[l], w_conv_out[l], q_norm_g[l], k_norm_g[l],
                   w_attn_out[l], w_out[l], norm2_g[l],
                   w_ffn_gate[l], w_ffn_up[l], w_ffn_down[l])
    return h
```

```python
import functools
import math

import jax
import jax.numpy as jnp
import ml_dtypes
import numpy as np
from jax import lax
from jax.experimental import pallas as pl
from jax.experimental.pallas import tpu as pltpu

F32 = jnp.float32
BF16 = jnp.bfloat16

N_HEADS = 8
MOBA_BLOCK = 256
MOBA_TOPK = 3
EPS = 1e-6
LOG2E = math.log2(math.e)
N_PIECES = 3
MASKED = -1e30
M_INIT = -1e29

SUBLANES = 8
LANES = 128
VT_PAD = 2 * SUBLANES
VMEM_LIMIT = 56 * 1024 * 1024
ATTN_TILE = 1024


def _const_spec(shape):
    nd = len(shape)
    return pl.BlockSpec(shape, lambda *_: (0,) * nd, pipeline_mode=pl.Buffered(1))


def _alibi_slope(h):
    return 2.0 ** (-8.0 * (h + 1) / N_HEADS)


def _bf16_pieces(x, n):
    pieces = []
    rest = np.float32(x)
    for _ in range(n):
        p = np.float32(rest.astype(ml_dtypes.bfloat16))
        pieces.append(float(p))
        rest = np.float32(rest - p)
    return pieces


def _dwconv_lanes(win_ref, sh_ref, y_ref, dww_ref, dwb_ref, lc, *, tm, kc, halo):
    off = halo - (kc - 1)
    span = sh_ref.shape[1]
    ls = slice(lc * LANES, (lc + 1) * LANES)
    for r in range(1, SUBLANES):
        sh_ref[r - 1] = win_ref[r:r + span, ls]
    rows = 64
    w_rows = [jnp.broadcast_to(dww_ref[k:k + 1, ls], (rows, LANES)) for k in range(kc)]
    bias = jnp.broadcast_to(dwb_ref[:, ls], (rows, LANES))
    for t0 in range(0, tm, rows):
        acc = bias
        for k in range(kc):
            q8, r = (off + k) // SUBLANES * SUBLANES, (off + k) % SUBLANES
            if r == 0:
                tap = win_ref[t0 + q8:t0 + q8 + rows, ls]
            else:
                tap = sh_ref[r - 1, t0 + q8:t0 + q8 + rows, :]
            acc = acc + tap * w_rows[k]
        y_ref[t0:t0 + rows, ls] = acc


def _in_proj_kernel(x_ref, g1_ref, w_ref, qg_ref, kg_ref, dww_ref, dwb_ref, lng_ref,
                    lnb_ref, act_ref, q_ref, k_ref, v_ref, sgc_ref, sga_ref, kmean_ref,
                    win_ref, sh_ref, y_ref, *, d, dh, scale, tiles_per_seq, kc, halo):
    tm = x_ref.shape[0]
    first = pl.program_id(0) % tiles_per_seq == 0

    @pl.when(first)
    def _():
        win_ref[0:halo] = jnp.zeros((halo, d), F32)

    @pl.when(jnp.logical_not(first))
    def _():
        win_ref[0:halo] = win_ref[tm:tm + halo]

    x = x_ref[...]
    n = x * lax.rsqrt(jnp.mean(x * x, axis=-1, keepdims=True) + EPS) * g1_ref[...]
    n = n.astype(BF16)

    def proj(c):
        return jnp.dot(n, w_ref[:, c * d:(c + 1) * d], preferred_element_type=F32)

    cu = proj(0)
    cg = proj(1)
    win_ref[halo:halo + tm] = cu * jax.nn.sigmoid(cg)
    for lc in range(d // LANES):
        _dwconv_lanes(win_ref, sh_ref, y_ref, dww_ref, dwb_ref, lc,
                      tm=tm, kc=kc, halo=halo)

    q = proj(2)
    for h in range(N_HEADS):
        qh = q[:, h * dh:(h + 1) * dh]
        qh = qh * lax.rsqrt(jnp.mean(qh * qh, axis=-1, keepdims=True) + EPS) * qg_ref[...]
        q_ref[:, h * dh:(h + 1) * dh] = (qh * (scale * LOG2E)).astype(BF16)
    k = proj(3)
    for h in range(N_HEADS):
        kh = k[:, h * dh:(h + 1) * dh]
        kh = kh * lax.rsqrt(jnp.mean(kh * kh, axis=-1, keepdims=True) + EPS) * kg_ref[...]
        k_ref[:, h * dh:(h + 1) * dh] = kh.astype(BF16)
        kmean_ref[0, :, h * dh:(h + 1) * dh] = jnp.mean(
            kh.reshape(tm // MOBA_BLOCK, MOBA_BLOCK, dh), axis=1)
    v_ref[...] = proj(4).astype(BF16)
    sgc_ref[...] = jax.nn.sigmoid(proj(5)).astype(BF16)
    sga_ref[...] = jax.nn.sigmoid(proj(6)).astype(BF16)

    y = y_ref[...]
    mu = jnp.mean(y, axis=-1, keepdims=True)
    yc = y - mu
    yn = yc * lax.rsqrt(jnp.mean(yc * yc, axis=-1, keepdims=True) + EPS)
    yn = yn * lng_ref[...] + lnb_ref[...]
    act_ref[...] = (yn * jax.nn.sigmoid(yn)).astype(BF16)


def _in_proj(x2, g1, w_in, qg, kg, dw_w, dw_b, ln_g, ln_b, *, tm, seq):
    t, d = x2.shape
    dh = d // N_HEADS
    nbt = tm // MOBA_BLOCK
    kc = dw_w.shape[0]
    halo = 32
    assert kc - 1 <= halo and halo % SUBLANES == 0 and seq % tm == 0
    off = halo - (kc - 1)
    span = tm + max((off + k) // SUBLANES * SUBLANES
                    for k in range(kc) if (off + k) % SUBLANES)
    assert SUBLANES - 1 + span <= tm + halo
    nt = t // tm
    row = pl.BlockSpec((tm, d), lambda i: (i, 0))
    vec = _const_spec((1, d))
    out_bf = jax.ShapeDtypeStruct((t, d), BF16)
    return pl.pallas_call(
        functools.partial(_in_proj_kernel, d=d, dh=dh, scale=dh ** -0.5,
                          tiles_per_seq=seq // tm, kc=kc, halo=halo),
        grid=(nt,),
        in_specs=[row, vec, _const_spec(w_in.shape),
                  _const_spec((1, dh)), _const_spec((1, dh)),
                  _const_spec(dw_w.shape), vec, vec, vec],
        out_specs=[row, row, row, row, row, row,
                   pl.BlockSpec((1, nbt, d), lambda i: (i, 0, 0))],
        out_shape=[out_bf] * 6 + [jax.ShapeDtypeStruct((nt, nbt, d), F32)],
        scratch_shapes=[pltpu.VMEM((tm + halo, d), F32),
                        pltpu.VMEM((SUBLANES - 1, span, LANES), F32),
                        pltpu.VMEM((tm, d), F32)],
        compiler_params=pltpu.CompilerParams(
            dimension_semantics=("arbitrary",), vmem_limit_bytes=VMEM_LIMIT),
        name="in_proj",
    )(x2, g1, w_in, qg, kg, dw_w, dw_b, ln_g, ln_b)


def _attn_prep_kernel(q_ref, k_ref, v_ref, kmean_ref, qt_ref, kf_ref, vt_ref,
                      *, dh, nb, tp):
    i = pl.program_id(1)
    base = i * tp
    nx = dh

    eye = (lax.broadcasted_iota(jnp.int32, (dh, dh), 0)
           == lax.broadcasted_iota(jnp.int32, (dh, dh), 1)).astype(BF16)
    nt = (((1,), (1,)), ((), ()))
    ones_rows = (lax.broadcasted_iota(jnp.int32, (VT_PAD, tp), 0) == 0).astype(BF16)

    blk = lax.broadcasted_iota(jnp.int32, (nb, tp), 0)
    qpos = base + lax.broadcasted_iota(jnp.int32, (nb, tp), 1)
    cur = qpos // MOBA_BLOCK
    past = blk < cur
    xrow = lax.broadcasted_iota(jnp.int32, (nx - nb, tp), 0)

    kcol = lax.broadcasted_iota(jnp.int32, (tp, nx), 1)
    kpos = base + lax.broadcasted_iota(jnp.int32, (tp, nx), 0)
    kblk = kpos // MOBA_BLOCK
    kpiece = kcol - nb
    is_pos = (kpiece >= 0) & (kpiece < 2 * N_PIECES)
    kx = jnp.where(kcol == kblk, 1.0, 0.0)
    kx = jnp.where(is_pos & (kpiece % 2 == 0), (MOBA_BLOCK * kblk).astype(F32), kx)
    kx = jnp.where(is_pos & (kpiece % 2 == 1), (kpos % MOBA_BLOCK).astype(F32), kx)
    kx = kx.astype(BF16)

    for h in range(N_HEADS):
        hs = slice(h * dh, (h + 1) * dh)
        qh = q_ref[0, :, hs]
        km = kmean_ref[0, :, hs]
        km_hi = km.astype(BF16)
        r1 = km - km_hi.astype(F32)
        km_mid = r1.astype(BF16)
        km_lo = (r1 - km_mid.astype(F32)).astype(BF16)
        lhs = jnp.concatenate([km_hi, km_mid, km_lo, eye], axis=0)
        r = lax.dot_general(lhs, qh, nt, preferred_element_type=F32)
        gate = r[0:nb] + r[nb:2 * nb] + r[2 * nb:3 * nb]
        q_t = r[3 * nb:3 * nb + dh]

        g = jnp.where(past, gate, -jnp.inf)
        sel = jnp.zeros((nb, tp), jnp.bool_)
        for _ in range(min(MOBA_TOPK, nb)):
            mx = jnp.max(g, axis=0, keepdims=True)
            first = jnp.min(jnp.where(g == mx, blk, nb), axis=0, keepdims=True)
            pick = blk == first
            sel = jnp.logical_or(sel, pick)
            g = jnp.where(pick, -jnp.inf, g)
        keep = jnp.logical_or(jnp.logical_and(sel, past), blk == cur)
        selbias = jnp.where(keep, 0.0, MASKED).astype(F32)

        xq = jnp.zeros((nx - nb, tp), F32)
        for t, c_t in enumerate(_bf16_pieces(_alibi_slope(h) * LOG2E, N_PIECES)):
            xq = jnp.where(xrow // 2 == t, c_t, xq)
        qt_ref[0, h] = jnp.concatenate([q_t, selbias, xq], axis=0).astype(BF16)
        kh = k_ref[0, :, hs]
        v_t = lax.dot_general(eye, v_ref[0, :, hs], nt,
                              preferred_element_type=F32).astype(BF16)
        v_t = jnp.concatenate([v_t, ones_rows], axis=0)
        for s in range(tp // MOBA_BLOCK):
            rs = slice(s * MOBA_BLOCK, (s + 1) * MOBA_BLOCK)
            kf_ref[0, h, s, :, 0:dh] = kh[rs]
            kf_ref[0, h, s, :, dh:2 * dh] = kx[rs]
            vt_ref[0, h, s] = v_t[:, rs]


def _attn_prep(q, k, v, kmean, *, tp):
    b, s, d = q.shape
    dh = d // N_HEADS
    nb = s // MOBA_BLOCK
    nsub = tp // MOBA_BLOCK
    assert dh + nb + 2 * N_PIECES <= 2 * dh
    row = pl.BlockSpec((1, tp, d), lambda bi, i: (bi, i, 0))
    return pl.pallas_call(
        functools.partial(_attn_prep_kernel, dh=dh, nb=nb, tp=tp),
        grid=(b, s // tp),
        in_specs=[row, row, row, pl.BlockSpec((1, nb, d), lambda bi, i: (bi, 0, 0))],
        out_specs=[
            pl.BlockSpec((1, N_HEADS, 2 * dh, tp), lambda bi, i: (bi, 0, 0, i)),
            pl.BlockSpec((1, N_HEADS, nsub, MOBA_BLOCK, 2 * dh),
                         lambda bi, i: (bi, 0, i, 0, 0)),
            pl.BlockSpec((1, N_HEADS, nsub, dh + VT_PAD, MOBA_BLOCK),
                         lambda bi, i: (bi, 0, i, 0, 0)),
        ],
        out_shape=[
            jax.ShapeDtypeStruct((b, N_HEADS, 2 * dh, s), BF16),
            jax.ShapeDtypeStruct((b, N_HEADS, nb, MOBA_BLOCK, 2 * dh), BF16),
            jax.ShapeDtypeStruct((b, N_HEADS, nb, dh + VT_PAD, MOBA_BLOCK), BF16),
        ],
        compiler_params=pltpu.CompilerParams(
            dimension_semantics=("arbitrary", "arbitrary"),
            vmem_limit_bytes=VMEM_LIMIT),
        name="attn_prep",
    )(q, k, v, kmean)


def _moba_attn_kernel(qt_ref, kf_ref, vt_ref, o_ref, s0_ref, s1_ref, s2_ref, s3_ref,
                      m_ref, acc_ref, *, tq):
    i = pl.program_id(2)
    blk = MOBA_BLOCK
    ngrp = tq // blk
    npast = i * ngrp
    m_ref[...] = jnp.full(m_ref.shape, M_INIT, F32)
    acc_ref[...] = jnp.zeros(acc_ref.shape, F32)

    s_refs = (s0_ref, s1_ref, s2_ref, s3_ref)

    def scores(j, c, buf):
        s_refs[buf][c] = jnp.dot(kf_ref[0, 0, j], qt_ref[0, 0, :, c * blk:(c + 1) * blk],
                                 preferred_element_type=F32)

    def probs(c, items):
        cs = slice(c * blk, (c + 1) * blk)

        def tile(buf, causal):
            s = s_refs[buf][c]
            if causal:
                kpos = lax.broadcasted_iota(jnp.int32, s.shape, 0)
                qpos = lax.broadcasted_iota(jnp.int32, s.shape, 1)
                s = jnp.where(kpos <= qpos, s, MASKED)
            return s

        m_old = m_ref[:, cs]
        m_new = m_old
        for _, buf, causal in items:
            m_new = jnp.maximum(m_new, jnp.max(tile(buf, causal), axis=0, keepdims=True))
        m_ref[:, cs] = m_new
        return (jnp.exp2(m_old - m_new),
                [jnp.exp2(tile(buf, causal) - m_new).astype(BF16) for _, buf, causal in items])

    def accumulate(c, alpha, ps, js):
        cs = slice(c * blk, (c + 1) * blk)
        pv = None
        for j, p in zip(js, ps):
            pj = jnp.dot(vt_ref[0, 0, j], p, preferred_element_type=F32)
            pv = pj if pv is None else pv + pj
        acc_ref[:, cs] = alpha * acc_ref[:, cs] + pv

    def absorb(c, items):
        alpha, ps = probs(c, items)
        accumulate(c, alpha, ps, [j for j, _, _ in items])

    for c in range(ngrp):
        scores(0, c, 0)
        scores(1, c, 1)

    def past_quad(t):
        for half in range(2):
            j = 4 * t + 2 * half
            cur, nxt = 2 * half, 2 * (1 - half)
            for c in range(ngrp):
                scores(j + 2, c, nxt)
                scores(j + 3, c, nxt + 1)
                absorb(c, [(j, cur, False), (j + 1, cur + 1, False)])

    def past_octet(t, carry):
        past_quad(2 * t)
        past_quad(2 * t + 1)
        return carry

    nquad = npast // 4
    lax.fori_loop(0, nquad // 2, past_octet, 0)

    @pl.when(nquad % 2 == 1)
    def _():
        past_quad(nquad - 1)

    for c in range(2, ngrp):
        for g in range(2, c + 1):
            scores(npast + g, c, g)
    for c in range(ngrp):
        own = [(npast + g, g, g == c) for g in range(c + 1)]
        for lo in range(0, len(own), 2):
            absorb(c, own[lo:lo + 2])
    dh = acc_ref.shape[0] - VT_PAD
    o_ref[0] = (acc_ref[0:dh, :] / acc_ref[dh:dh + 1, :]).T.astype(o_ref.dtype)


def _moba_attn(qt, kf, vt, *, s, d, tq):
    b, nh, dq, _ = qt.shape
    nb = kf.shape[2]
    dh = d // nh
    blk = MOBA_BLOCK
    assert tq // blk == 4
    sbuf = pltpu.VMEM((tq // blk, blk, blk), F32)
    return pl.pallas_call(
        functools.partial(_moba_attn_kernel, tq=tq),
        grid=(b, nh, s // tq),
        in_specs=[
            pl.BlockSpec((1, 1, dq, tq), lambda bi, h, i: (bi, h, 0, i)),
            pl.BlockSpec((1, 1, nb, blk, dq), lambda bi, h, i: (bi, h, 0, 0, 0)),
            pl.BlockSpec((1, 1, nb, dh + VT_PAD, blk), lambda bi, h, i: (bi, h, 0, 0, 0)),
        ],
        out_specs=pl.BlockSpec((1, tq, dh), lambda bi, h, i: (bi, i, h)),
        out_shape=jax.ShapeDtypeStruct((b, s, d), BF16),
        scratch_shapes=[sbuf, sbuf, sbuf, sbuf,
                        pltpu.VMEM((1, tq), F32), pltpu.VMEM((dh + VT_PAD, tq), F32)],
        compiler_params=pltpu.CompilerParams(
            dimension_semantics=("arbitrary", "arbitrary", "arbitrary"),
            vmem_limit_bytes=VMEM_LIMIT),
        name="moba_attn",
    )(qt, kf, vt)


def _merge_ffn_kernel(act_ref, att_ref, sgc_ref, sga_ref, x_ref, wco_ref, wao_ref,
                      wo_ref, g2_ref, wg_ref, wu_ref, wd_ref, o_ref, *, chunks):
    y_conv = jnp.dot(act_ref[...], wco_ref[...], preferred_element_type=F32)
    y_attn = jnp.dot(att_ref[...], wao_ref[...], preferred_element_type=F32)
    merged = sgc_ref[...].astype(F32) * y_conv + sga_ref[...].astype(F32) * y_attn
    h = x_ref[...] + jnp.dot(merged.astype(BF16), wo_ref[...],
                             preferred_element_type=F32)
    n = h * lax.rsqrt(jnp.mean(h * h, axis=-1, keepdims=True) + EPS) * g2_ref[...]
    n = n.astype(BF16)
    out = h
    for lo, hi in chunks:
        g = jnp.dot(n, wg_ref[:, lo:hi], preferred_element_type=F32)
        u = jnp.dot(n, wu_ref[:, lo:hi], preferred_element_type=F32)
        a = (g * jax.nn.sigmoid(g) * u).astype(BF16)
        out = out + jnp.dot(a, wd_ref[lo:hi, :], preferred_element_type=F32)
    o_ref[...] = out


def _merge_ffn(act, att, sgc, sga, x2, w_co, w_ao, w_o, g2, w_g, w_u, w_d, *, tm):
    t, d = x2.shape
    dff = w_g.shape[1]
    step = 768
    chunks = tuple((lo, min(lo + step, dff)) for lo in range(0, dff, step))
    row = pl.BlockSpec((tm, d), lambda i: (i, 0))
    return pl.pallas_call(
        functools.partial(_merge_ffn_kernel, chunks=chunks),
        grid=(t // tm,),
        in_specs=[row, row, row, row, row, _const_spec(w_co.shape),
                  _const_spec(w_ao.shape), _const_spec(w_o.shape), _const_spec((1, d)),
                  _const_spec(w_g.shape), _const_spec(w_u.shape), _const_spec(w_d.shape)],
        out_specs=row,
        out_shape=jax.ShapeDtypeStruct((t, d), F32),
        compiler_params=pltpu.CompilerParams(
            dimension_semantics=("arbitrary",), vmem_limit_bytes=VMEM_LIMIT),
        name="merge_ffn",
    )(act, att, sgc, sga, x2, w_co, w_ao, w_o, g2, w_g, w_u, w_d)


def _layer(x, norm1_g, w_in, dw_w, dw_b, conv_ln_g, conv_ln_b, w_conv_out,
           q_norm_g, k_norm_g, w_attn_out, w_out, norm2_g,
           w_ffn_gate, w_ffn_up, w_ffn_down):
    b, s, d = x.shape
    t = b * s
    tm = 512
    row = lambda p: p.reshape(1, -1)
    bf = lambda w: w.astype(BF16)

    x2 = x.reshape(t, d)
    act, q, k, v, sgc, sga, kmean = _in_proj(
        x2, row(norm1_g), bf(w_in), row(q_norm_g), row(k_norm_g),
        dw_w, row(dw_b), row(conv_ln_g), row(conv_ln_b), tm=tm, seq=s)
    r3 = lambda z: z.reshape(b, s, d)
    kmean = kmean.reshape(b, s // MOBA_BLOCK, d)
    qt, kf, vt = _attn_prep(r3(q), r3(k), r3(v), kmean, tp=ATTN_TILE)
    att = _moba_attn(qt, kf, vt, s=s, d=d, tq=ATTN_TILE)
    out = _merge_ffn(act, att.reshape(t, d), sgc, sga, x2, bf(w_conv_out),
                     bf(w_attn_out), bf(w_out), row(norm2_g), bf(w_ffn_gate),
                     bf(w_ffn_up), bf(w_ffn_down), tm=tm)
    return out.reshape(b, s, d)


def kernel(x, norm1_g, w_in, dw_w, dw_b, conv_ln_g, conv_ln_b, w_conv_out,
           q_norm_g, k_norm_g, w_attn_out, w_out, norm2_g,
           w_ffn_gate, w_ffn_up, w_ffn_down):
    h = x
    for l in range(norm1_g.shape[0]):
        h = _layer(h, norm1_g[l], w_in[l], dw_w[l], dw_b[l], conv_ln_g[l],
                   conv_ln_b[l], w_conv_out[l], q_norm_g[l], k_norm_g[l],
                   w_attn_out[l], w_out[l], norm2_g[l],
                   w_ffn_gate[l], w_ffn_up[l], w_ffn_down[l])
    return h
```

```python
import functools
import math

import jax
import jax.numpy as jnp
import ml_dtypes
import numpy as np
from jax import lax
from jax.experimental import pallas as pl
from jax.experimental.pallas import tpu as pltpu

F32 = jnp.float32
BF16 = jnp.bfloat16

N_HEADS = 8
MOBA_BLOCK = 256
MOBA_TOPK = 3
EPS = 1e-6
LOG2E = math.log2(math.e)
N_PIECES = 3
MASKED = -1e30
M_INIT = -1e29

SUBLANES = 8
LANES = 128
VT_PAD = 2 * SUBLANES
VMEM_LIMIT = 56 * 1024 * 1024
ATTN_TILE = 1024


def _const_spec(shape):
    nd = len(shape)
    return pl.BlockSpec(shape, lambda *_: (0,) * nd, pipeline_mode=pl.Buffered(1))


def _alibi_slope(h):
    return 2.0 ** (-8.0 * (h + 1) / N_HEADS)


def _bf16_pieces(x, n):
    pieces = []
    rest = np.float32(x)
    for _ in range(n):
        p = np.float32(rest.astype(ml_dtypes.bfloat16))
        pieces.append(float(p))
        rest = np.float32(rest - p)
    return pieces


def _dwconv_lanes(win_ref, sh_ref, y_ref, dww_ref, dwb_ref, lc, *, tm, kc, halo):
    off = halo - (kc - 1)
    span = sh_ref.shape[1]
    ls = slice(lc * LANES, (lc + 1) * LANES)
    for r in range(1, SUBLANES):
        sh_ref[r - 1] = win_ref[r:r + span, ls]
    rows = 64
    w_rows = [jnp.broadcast_to(dww_ref[k:k + 1, ls], (rows, LANES)) for k in range(kc)]
    bias = jnp.broadcast_to(dwb_ref[:, ls], (rows, LANES))
    for t0 in range(0, tm, rows):
        acc = bias
        for k in range(kc):
            q8, r = (off + k) // SUBLANES * SUBLANES, (off + k) % SUBLANES
            if r == 0:
                tap = win_ref[t0 + q8:t0 + q8 + rows, ls]
            else:
                tap = sh_ref[r - 1, t0 + q8:t0 + q8 + rows, :]
            acc = acc + tap * w_rows[k]
        y_ref[t0:t0 + rows, ls] = acc


def _in_proj_kernel(x_ref, g1_ref, w_ref, qg_ref, kg_ref, dww_ref, dwb_ref, lng_ref,
                    lnb_ref, act_ref, q_ref, k_ref, v_ref, sgc_ref, sga_ref, kmean_ref,
                    win_ref, sh_ref, y_ref, *, d, dh, scale, tiles_per_seq, kc, halo):
    tm = x_ref.shape[0]
    first = pl.program_id(0) % tiles_per_seq == 0

    @pl.when(first)
    def _():
        win_ref[0:halo] = jnp.zeros((halo, d), F32)

    @pl.when(jnp.logical_not(first))
    def _():
        win_ref[0:halo] = win_ref[tm:tm + halo]

    x = x_ref[...]
    n = x * lax.rsqrt(jnp.mean(x * x, axis=-1, keepdims=True) + EPS) * g1_ref[...]
    n = n.astype(BF16)

    def proj(c):
        return jnp.dot(n, w_ref[:, c * d:(c + 1) * d], preferred_element_type=F32)

    cu = proj(0)
    cg = proj(1)
    win_ref[halo:halo + tm] = cu * jax.nn.sigmoid(cg)
    for lc in range(d // LANES):
        _dwconv_lanes(win_ref, sh_ref, y_ref, dww_ref, dwb_ref, lc,
                      tm=tm, kc=kc, halo=halo)

    q = proj(2)
    for h in range(N_HEADS):
        qh = q[:, h * dh:(h + 1) * dh]
        qh = qh * lax.rsqrt(jnp.mean(qh * qh, axis=-1, keepdims=True) + EPS) * qg_ref[...]
        q_ref[:, h * dh:(h + 1) * dh] = (qh * (scale * LOG2E)).astype(BF16)
    k = proj(3)
    for h in range(N_HEADS):
        kh = k[:, h * dh:(h + 1) * dh]
        kh = kh * lax.rsqrt(jnp.mean(kh * kh, axis=-1, keepdims=True) + EPS) * kg_ref[...]
        k_ref[:, h * dh:(h + 1) * dh] = kh.astype(BF16)
        kmean_ref[0, :, h * dh:(h + 1) * dh] = jnp.mean(
            kh.reshape(tm // MOBA_BLOCK, MOBA_BLOCK, dh), axis=1)
    v_ref[...] = proj(4).astype(BF16)
    sgc_ref[...] = jax.nn.sigmoid(proj(5)).astype(BF16)
    sga_ref[...] = jax.nn.sigmoid(proj(6)).astype(BF16)

    y = y_ref[...]
    mu = jnp.mean(y, axis=-1, keepdims=True)
    yc = y - mu
    yn = yc * lax.rsqrt(jnp.mean(yc * yc, axis=-1, keepdims=True) + EPS)
    yn = yn * lng_ref[...] + lnb_ref[...]
    act_ref[...] = (yn * jax.nn.sigmoid(yn)).astype(BF16)


def _in_proj(x2, g1, w_in, qg, kg, dw_w, dw_b, ln_g, ln_b, *, tm, seq):
    t, d = x2.shape
    dh = d // N_HEADS
    nbt = tm // MOBA_BLOCK
    kc = dw_w.shape[0]
    halo = 32
    assert kc - 1 <= halo and halo % SUBLANES == 0 and seq % tm == 0
    off = halo - (kc - 1)
    span = tm + max((off + k) // SUBLANES * SUBLANES
                    for k in range(kc) if (off + k) % SUBLANES)
    assert SUBLANES - 1 + span <= tm + halo
    nt = t // tm
    row = pl.BlockSpec((tm, d), lambda i: (i, 0))
    vec = _const_spec((1, d))
    out_bf = jax.ShapeDtypeStruct((t, d), BF16)
    return pl.pallas_call(
        functools.partial(_in_proj_kernel, d=d, dh=dh, scale=dh ** -0.5,
                          tiles_per_seq=seq // tm, kc=kc, halo=halo),
        grid=(nt,),
        in_specs=[row, vec, _const_spec(w_in.shape),
                  _const_spec((1, dh)), _const_spec((1, dh)),
                  _const_spec(dw_w.shape), vec, vec, vec],
        out_specs=[row, row, row, row, row, row,
                   pl.BlockSpec((1, nbt, d), lambda i: (i, 0, 0))],
        out_shape=[out_bf] * 6 + [jax.ShapeDtypeStruct((nt, nbt, d), F32)],
        scratch_shapes=[pltpu.VMEM((tm + halo, d), F32),
                        pltpu.VMEM((SUBLANES - 1, span, LANES), F32),
                        pltpu.VMEM((tm, d), F32)],
        compiler_params=pltpu.CompilerParams(
            dimension_semantics=("arbitrary",), vmem_limit_bytes=VMEM_LIMIT),
        name="in_proj",
    )(x2, g1, w_in, qg, kg, dw_w, dw_b, ln_g, ln_b)


def _attn_prep_kernel(q_ref, k_ref, v_ref, kmean_ref, qt_ref, kf_ref, vt_ref,
                      *, dh, nb, tp):
    i = pl.program_id(1)
    base = i * tp
    nx = dh

    eye = (lax.broadcasted_iota(jnp.int32, (dh, dh), 0)
           == lax.broadcasted_iota(jnp.int32, (dh, dh), 1)).astype(BF16)
    nt = (((1,), (1,)), ((), ()))
    ones_rows = (lax.broadcasted_iota(jnp.int32, (VT_PAD, tp), 0) == 0).astype(BF16)

    blk = lax.broadcasted_iota(jnp.int32, (nb, tp), 0)
    qpos = base + lax.broadcasted_iota(jnp.int32, (nb, tp), 1)
    cur = qpos // MOBA_BLOCK
    past = blk < cur
    xrow = lax.broadcasted_iota(jnp.int32, (nx - nb, tp), 0)

    kcol = lax.broadcasted_iota(jnp.int32, (tp, nx), 1)
    kpos = base + lax.broadcasted_iota(jnp.int32, (tp, nx), 0)
    kblk = kpos // MOBA_BLOCK
    kpiece = kcol - nb
    is_pos = (kpiece >= 0) & (kpiece < 2 * N_PIECES)
    kx = jnp.where(kcol == kblk, 1.0, 0.0)
    kx = jnp.where(is_pos & (kpiece % 2 == 0), (MOBA_BLOCK * kblk).astype(F32), kx)
    kx = jnp.where(is_pos & (kpiece % 2 == 1), (kpos % MOBA_BLOCK).astype(F32), kx)
    kx = kx.astype(BF16)

    for h in range(N_HEADS):
        hs = slice(h * dh, (h + 1) * dh)
        qh = q_ref[0, :, hs]
        km = kmean_ref[0, :, hs]
        km_hi = km.astype(BF16)
        r1 = km - km_hi.astype(F32)
        km_mid = r1.astype(BF16)
        km_lo = (r1 - km_mid.astype(F32)).astype(BF16)
        lhs = jnp.concatenate([km_hi, km_mid, km_lo, eye], axis=0)
        r = lax.dot_general(lhs, qh, nt, preferred_element_type=F32)
        gate = r[0:nb] + r[nb:2 * nb] + r[2 * nb:3 * nb]
        q_t = r[3 * nb:3 * nb + dh]

        g = jnp.where(past, gate, -jnp.inf)
        sel = jnp.zeros((nb, tp), jnp.bool_)
        for _ in range(min(MOBA_TOPK, nb)):
            mx = jnp.max(g, axis=0, keepdims=True)
            first = jnp.min(jnp.where(g == mx, blk, nb), axis=0, keepdims=True)
            pick = blk == first
            sel = jnp.logical_or(sel, pick)
            g = jnp.where(pick, -jnp.inf, g)
        keep = jnp.logical_or(jnp.logical_and(sel, past), blk == cur)
        selbias = jnp.where(keep, 0.0, MASKED).astype(F32)

        xq = jnp.zeros((nx - nb, tp), F32)
        for t, c_t in enumerate(_bf16_pieces(_alibi_slope(h) * LOG2E, N_PIECES)):
            xq = jnp.where(xrow // 2 == t, c_t, xq)
        qt_ref[0, h] = jnp.concatenate([q_t, selbias, xq], axis=0).astype(BF16)
        kh = k_ref[0, :, hs]
        v_t = lax.dot_general(eye, v_ref[0, :, hs], nt,
                              preferred_element_type=F32).astype(BF16)
        v_t = jnp.concatenate([v_t, ones_rows], axis=0)
        for s in range(tp // MOBA_BLOCK):
            rs = slice(s * MOBA_BLOCK, (s + 1) * MOBA_BLOCK)
            kf_ref[0, h, s, :, 0:dh] = kh[rs]
            kf_ref[0, h, s, :, dh:2 * dh] = kx[rs]
            vt_ref[0, h, s] = v_t[:, rs]


def _attn_prep(q, k, v, kmean, *, tp):
    b, s, d = q.shape
    dh = d // N_HEADS
    nb = s // MOBA_BLOCK
    nsub = tp // MOBA_BLOCK
    assert dh + nb + 2 * N_PIECES <= 2 * dh
    row = pl.BlockSpec((1, tp, d), lambda bi, i: (bi, i, 0))
    return pl.pallas_call(
        functools.partial(_attn_prep_kernel, dh=dh, nb=nb, tp=tp),
        grid=(b, s // tp),
        in_specs=[row, row, row, pl.BlockSpec((1, nb, d), lambda bi, i: (bi, 0, 0))],
        out_specs=[
            pl.BlockSpec((1, N_HEADS, 2 * dh, tp), lambda bi, i: (bi, 0, 0, i)),
            pl.BlockSpec((1, N_HEADS, nsub, MOBA_BLOCK, 2 * dh),
                         lambda bi, i: (bi, 0, i, 0, 0)),
            pl.BlockSpec((1, N_HEADS, nsub, dh + VT_PAD, MOBA_BLOCK),
                         lambda bi, i: (bi, 0, i, 0, 0)),
        ],
        out_shape=[
            jax.ShapeDtypeStruct((b, N_HEADS, 2 * dh, s), BF16),
            jax.ShapeDtypeStruct((b, N_HEADS, nb, MOBA_BLOCK, 2 * dh), BF16),
            jax.ShapeDtypeStruct((b, N_HEADS, nb, dh + VT_PAD, MOBA_BLOCK), BF16),
        ],
        compiler_params=pltpu.CompilerParams(
            dimension_semantics=("arbitrary", "arbitrary"),
            vmem_limit_bytes=VMEM_LIMIT),
        name="attn_prep",
    )(q, k, v, kmean)


def _moba_attn_kernel(qt_ref, kf_ref, vt_ref, o_ref, s0_ref, s1_ref, s2_ref, s3_ref,
                      m_ref, acc_ref, *, tq):
    i = pl.program_id(2)
    blk = MOBA_BLOCK
    ngrp = tq // blk
    npast = i * ngrp
    m_ref[...] = jnp.full(m_ref.shape, M_INIT, F32)
    acc_ref[...] = jnp.zeros(acc_ref.shape, F32)

    s_refs = (s0_ref, s1_ref, s2_ref, s3_ref)

    def scores(j, c, buf):
        s_refs[buf][c] = jnp.dot(kf_ref[0, 0, j], qt_ref[0, 0, :, c * blk:(c + 1) * blk],
                                 preferred_element_type=F32)

    def probs(c, items):
        cs = slice(c * blk, (c + 1) * blk)

        def tile(buf, causal):
            s = s_refs[buf][c]
            if causal:
                kpos = lax.broadcasted_iota(jnp.int32, s.shape, 0)
                qpos = lax.broadcasted_iota(jnp.int32, s.shape, 1)
                s = jnp.where(kpos <= qpos, s, MASKED)
            return s

        m_old = m_ref[:, cs]
        m_new = m_old
        for _, buf, causal in items:
            m_new = jnp.maximum(m_new, jnp.max(tile(buf, causal), axis=0, keepdims=True))
        m_ref[:, cs] = m_new
        return (jnp.exp2(m_old - m_new),
                [jnp.exp2(tile(buf, causal) - m_new).astype(BF16) for _, buf, causal in items])

    def accumulate(c, alpha, ps, js):
        cs = slice(c * blk, (c + 1) * blk)
        pv = None
        for j, p in zip(js, ps):
            pj = jnp.dot(vt_ref[0, 0, j], p, preferred_element_type=F32)
            pv = pj if pv is None else pv + pj
        acc_ref[:, cs] = alpha * acc_ref[:, cs] + pv

    def absorb(c, items):
        alpha, ps = probs(c, items)
        accumulate(c, alpha, ps, [j for j, _, _ in items])

    for c in range(ngrp):
        scores(0, c, 0)
        scores(1, c, 1)

    def past_quad(t):
        for half in range(2):
            j = 4 * t + 2 * half
            cur, nxt = 2 * half, 2 * (1 - half)
            for c in range(ngrp):
                scores(j + 2, c, nxt)
                scores(j + 3, c, nxt + 1)
                absorb(c, [(j, cur, False), (j + 1, cur + 1, False)])

    unroll = 4

    def past_quads(t, carry):
        for u in range(unroll):
            past_quad(unroll * t + u)
        return carry

    nquad = npast // 4
    lax.fori_loop(0, nquad // unroll, past_quads, 0)
    for u in range(unroll - 1):
        @pl.when(nquad % unroll > u)
        def _():
            past_quad(nquad // unroll * unroll + u)

    for c in range(2, ngrp):
        for g in range(2, c + 1):
            scores(npast + g, c, g)
    for c in range(ngrp):
        own = [(npast + g, g, g == c) for g in range(c + 1)]
        for lo in range(0, len(own), 2):
            absorb(c, own[lo:lo + 2])
    dh = acc_ref.shape[0] - VT_PAD
    o_ref[0] = (acc_ref[0:dh, :] / acc_ref[dh:dh + 1, :]).T.astype(o_ref.dtype)


def _moba_attn(qt, kf, vt, *, s, d, tq):
    b, nh, dq, _ = qt.shape
    nb = kf.shape[2]
    dh = d // nh
    blk = MOBA_BLOCK
    assert tq // blk == 4
    sbuf = pltpu.VMEM((tq // blk, blk, blk), F32)
    return pl.pallas_call(
        functools.partial(_moba_attn_kernel, tq=tq),
        grid=(b, nh, s // tq),
        in_specs=[
            pl.BlockSpec((1, 1, dq, tq), lambda bi, h, i: (bi, h, 0, i)),
            pl.BlockSpec((1, 1, nb, blk, dq), lambda bi, h, i: (bi, h, 0, 0, 0)),
            pl.BlockSpec((1, 1, nb, dh + VT_PAD, blk), lambda bi, h, i: (bi, h, 0, 0, 0)),
        ],
        out_specs=pl.BlockSpec((1, tq, dh), lambda bi, h, i: (bi, i, h)),
        out_shape=jax.ShapeDtypeStruct((b, s, d), BF16),
        scratch_shapes=[sbuf, sbuf, sbuf, sbuf,
                        pltpu.VMEM((1, tq), F32), pltpu.VMEM((dh + VT_PAD, tq), F32)],
        compiler_params=pltpu.CompilerParams(
            dimension_semantics=("arbitrary", "arbitrary", "arbitrary"),
            vmem_limit_bytes=VMEM_LIMIT),
        name="moba_attn",
    )(qt, kf, vt)


def _merge_ffn_kernel(act_ref, att_ref, sgc_ref, sga_ref, x_ref, wco_ref, wao_ref,
                      wo_ref, g2_ref, wg_ref, wu_ref, wd_ref, o_ref, *, chunks):
    y_conv = jnp.dot(act_ref[...], wco_ref[...], preferred_element_type=F32)
    y_attn = jnp.dot(att_ref[...], wao_ref[...], preferred_element_type=F32)
    merged = sgc_ref[...].astype(F32) * y_conv + sga_ref[...].astype(F32) * y_attn
    h = x_ref[...] + jnp.dot(merged.astype(BF16), wo_ref[...],
                             preferred_element_type=F32)
    n = h * lax.rsqrt(jnp.mean(h * h, axis=-1, keepdims=True) + EPS) * g2_ref[...]
    n = n.astype(BF16)
    out = h
    for lo, hi in chunks:
        g = jnp.dot(n, wg_ref[:, lo:hi], preferred_element_type=F32)
        u = jnp.dot(n, wu_ref[:, lo:hi], preferred_element_type=F32)
        a = (g * jax.nn.sigmoid(g) * u).astype(BF16)
        out = out + jnp.dot(a, wd_ref[lo:hi, :], preferred_element_type=F32)
    o_ref[...] = out


def _merge_ffn(act, att, sgc, sga, x2, w_co, w_ao, w_o, g2, w_g, w_u, w_d, *, tm):
    t, d = x2.shape
    dff = w_g.shape[1]
    step = 768
    chunks = tuple((lo, min(lo + step, dff)) for lo in range(0, dff, step))
    row = pl.BlockSpec((tm, d), lambda i: (i, 0))
    return pl.pallas_call(
        functools.partial(_merge_ffn_kernel, chunks=chunks),
        grid=(t // tm,),
        in_specs=[row, row, row, row, row, _const_spec(w_co.shape),
                  _const_spec(w_ao.shape), _const_spec(w_o.shape), _const_spec((1, d)),
                  _const_spec(w_g.shape), _const_spec(w_u.shape), _const_spec(w_d.shape)],
        out_specs=row,
        out_shape=jax.ShapeDtypeStruct((t, d), F32),
        compiler_params=pltpu.CompilerParams(
            dimension_semantics=("arbitrary",), vmem_limit_bytes=VMEM_LIMIT),
        name="merge_ffn",
    )(act, att, sgc, sga, x2, w_co, w_ao, w_o, g2, w_g, w_u, w_d)


def _layer(x, norm1_g, w_in, dw_w, dw_b, conv_ln_g, conv_ln_b, w_conv_out,
           q_norm_g, k_norm_g, w_attn_out, w_out, norm2_g,
           w_ffn_gate, w_ffn_up, w_ffn_down):
    b, s, d = x.shape
    t = b * s
    tm = 512
    row = lambda p: p.reshape(1, -1)
    bf = lambda w: w.astype(BF16)

    x2 = x.reshape(t, d)
    act, q, k, v, sgc, sga, kmean = _in_proj(
        x2, row(norm1_g), bf(w_in), row(q_norm_g), row(k_norm_g),
        dw_w, row(dw_b), row(conv_ln_g), row(conv_ln_b), tm=tm, seq=s)
    r3 = lambda z: z.reshape(b, s, d)
    kmean = kmean.reshape(b, s // MOBA_BLOCK, d)
    qt, kf, vt = _attn_prep(r3(q), r3(k), r3(v), kmean, tp=ATTN_TILE)
    att = _moba_attn(qt, kf, vt, s=s, d=d, tq=ATTN_TILE)
    out = _merge_ffn(act, att.reshape(t, d), sgc, sga, x2, bf(w_conv_out),
                     bf(w_attn_out), bf(w_out), row(norm2_g), bf(w_ffn_gate),
                     bf(w_ffn_up), bf(w_ffn_down), tm=tm)
    return out.reshape(b, s, d)


def kernel(x, norm1_g, w_in, dw_w, dw_b, conv_ln_g, conv_ln_b, w_conv_out,
           q_norm_g, k_norm_g, w_attn_out, w_out, norm2_g,
           w_ffn_gate, w_ffn_up, w_ffn_down):
    h = x
    for l in range(norm1_g.shape[0]):
        h = _layer(h, norm1_g[l], w_in[l], dw_w[l], dw_b[l], conv_ln_g[l],
                   conv_ln_b[l], w_conv_out[l], q_norm_g[l], k_norm_g[l],
                   w_attn_out[l], w_out[l], norm2_g[l],
                   w_ffn_gate[l], w_ffn_up[l], w_ffn_down[l])
    return h
```
